```python
import math
import jax, jax.numpy as jnp
from jax import lax
import numpy as np

D_MODEL = 1024
BATCH = 2
SEQ = 8192
DEPTH = 2

HG_HEADS = 4
HG_DK = 128
HG_DV = 128
HG_WIDTH = HG_HEADS * HG_DK
HG_CHUNK = 16
HG_F_MIN = 1e-6
S5_WIDTH = 512
S5_GROUP = 16
S5_GROUPS = S5_WIDTH // S5_GROUP
S5_STATE = 64
RET_HEADS = 4
RET_DK = 128
RET_DV = 128
RET_WIDTH = RET_HEADS * RET_DK
RET_CHUNK = 128
ROPE_BASE = 10000.0
N_BRANCH = 3
MIX_WIDTH = HG_WIDTH + S5_WIDTH + RET_WIDTH
IN_COLS = 4 * HG_WIDTH + S5_WIDTH + 4 * RET_WIDTH + N_BRANCH * D_MODEL
MEM_LEN = 256
XA_HEADS = 4
XA_DH = D_MODEL // XA_HEADS
PEER_HEADS = 8
PEER_NKEYS = 128
PEER_EXPERTS = PEER_NKEYS * PEER_NKEYS
PEER_DQ = 256
PEER_DH = PEER_DQ // 2
PEER_TOPK = 16
PEER_BLOCK = 128
EPS = 1e-6
F32 = jnp.float32

kernel_name = 'hybrid_hgrn2_s5_retnet_peer_block'


def rmsnorm(x, gain):
    xf = x.astype(F32)
    y = xf * lax.rsqrt(jnp.mean(xf * xf, axis=-1, keepdims=True) + EPS)
    return (y * gain.astype(F32)).astype(x.dtype)


def _to_chunks(a, n_heads, chunk):
    b, length, width = a.shape
    return a.reshape(b, length // chunk, chunk, n_heads, width // n_heads).transpose(0, 3, 1, 2, 4)


def _from_chunks(a):
    b, h, n, c, d = a.shape
    return a.transpose(0, 2, 3, 1, 4).reshape(b, n * c, h, d)


def _inter_chunk(q_dec, k_dec, v, chunk_decay):
    bsz, h, _, _, dk = q_dec.shape
    dv = v.shape[-1]
    xs = (jnp.moveaxis(q_dec, 2, 0), jnp.moveaxis(k_dec, 2, 0), jnp.moveaxis(v, 2, 0), jnp.moveaxis(chunk_decay, 2, 0))

    def step(state, inp):
        qd, kd, vv, dec = inp
        out = jnp.einsum('bhck,bhkv->bhcv', qd, state)
        state = dec[..., None] * state + jnp.einsum('bhck,bhcv->bhkv', kd, vv)
        return state, out

    s0 = jnp.zeros((bsz, h, dk, dv), q_dec.dtype)
    _, out = lax.scan(step, s0, xs)
    return jnp.moveaxis(out, 0, 2)


def hgrn2(q, f_logit, inp, gate, lb, norm_g):
    bsz, length, _ = q.shape
    h, dk, dv, c = HG_HEADS, HG_DK, HG_DV, HG_CHUNK
    z = f_logit.astype(F32)
    lbf = lb.astype(F32)
    f = lbf + (1.0 - lbf) * jax.nn.sigmoid(z)
    log_f = jnp.log(jnp.clip(f, HG_F_MIN, 1.0))
    k = (1.0 - lbf) * jax.nn.sigmoid(-z)
    qc = _to_chunks(q.astype(F32) * dk ** -0.5, h, c)
    kc = _to_chunks(k, h, c)
    vc = _to_chunks(inp.astype(F32), h, c)
    b = jnp.cumsum(_to_chunks(log_f, h, c), axis=3)
    causal = jnp.tril(jnp.ones((c, c), bool))[:, :, None]
    diff = b[..., :, None, :] - b[..., None, :, :]
    decay = jnp.where(causal, jnp.exp(jnp.minimum(diff, 0.0)), 0.0)
    att = jnp.einsum('bhntk,bhntsk,bhnsk->bhnts', qc, decay, kc)
    o = jnp.einsum('bhnts,bhnsv->bhntv', att, vc)
    b_last = b[..., -1:, :]
    o = o + _inter_chunk(qc * jnp.exp(b), kc * jnp.exp(b_last - b), vc, jnp.exp(b_last[..., 0, :]))
    o = _from_chunks(o)
    o = o * lax.rsqrt(jnp.mean(o * o, axis=-1, keepdims=True) + EPS) * norm_g.astype(F32).reshape(h, dv)
    return o.reshape(bsz, length, h * dv) * jax.nn.silu(gate.astype(F32))


def s5(u, lam_re, lam_im, log_step, b_re, b_im, c_re, c_im, d, w_glu, b_glu):
    bsz, length, _ = u.shape
    uf = u.astype(F32).reshape(bsz, length, S5_GROUPS, S5_GROUP)
    lr = jnp.minimum(lam_re.astype(F32), -1e-4)
    li = lam_im.astype(F32)
    dt = jnp.exp(log_step.astype(F32))[:, None]
    mag = jnp.exp(lr * dt)
    a_re = mag * jnp.cos(li * dt)
    a_im = mag * jnp.sin(li * dt)
    den = lr * lr + li * li
    z_re = ((a_re - 1.0) * lr + a_im * li) / den
    z_im = (a_im * lr - (a_re - 1.0) * li) / den
    bb_re = z_re[..., None] * b_re.astype(F32) - z_im[..., None] * b_im.astype(F32)
    bb_im = z_re[..., None] * b_im.astype(F32) + z_im[..., None] * b_re.astype(F32)
    bu_re = jnp.einsum('blgc,gpc->blgp', uf, bb_re)
    bu_im = jnp.einsum('blgc,gpc->blgp', uf, bb_im)
    ar_full = jnp.broadcast_to(a_re, bu_re.shape)
    ai_full = jnp.broadcast_to(a_im, bu_re.shape)

    def combine(e1, e2):
        ar1, ai1, br1, bi1 = e1
        ar2, ai2, br2, bi2 = e2
        return (ar2 * ar1 - ai2 * ai1, ar2 * ai1 + ai2 * ar1,
                ar2 * br1 - ai2 * bi1 + br2, ar2 * bi1 + ai2 * br1 + bi2)

    _, _, x_re, x_im = lax.associative_scan(combine, (ar_full, ai_full, bu_re, bu_im), axis=1)
    y = (jnp.einsum('blgp,gcp->blgc', x_re, c_re.astype(F32))
         - jnp.einsum('blgp,gcp->blgc', x_im, c_im.astype(F32))
         + d.astype(F32).reshape(S5_GROUPS, S5_GROUP) * uf)
    y = jax.nn.gelu(y.reshape(bsz, length, S5_WIDTH))
    return y * jax.nn.sigmoid(y @ w_glu.astype(F32) + b_glu.astype(F32))


def rope(x, positions):
    half = x.shape[-1] // 2
    inv = ROPE_BASE ** (-jnp.arange(half, dtype=F32) / half)
    ang = positions.astype(F32)[:, :, None, None] * inv
    cos, sin = jnp.cos(ang), jnp.sin(ang)
    x1, x2 = x[..., :half], x[..., half:]
    return jnp.concatenate([x1 * cos - x2 * sin, x1 * sin + x2 * cos], axis=-1)


def retention(q, k, v, gate, positions):
    bsz, length, _ = q.shape
    h, dk, dv, c = RET_HEADS, RET_DK, RET_DV, RET_CHUNK
    n = length // c
    qh = rope(q.astype(F32).reshape(bsz, length, h, dk), positions).reshape(bsz, length, h * dk)
    kh = (rope(k.astype(F32).reshape(bsz, length, h, dk), positions) * dk ** -0.5).reshape(bsz, length, h * dk)
    qc = _to_chunks(qh, h, c)
    kc = _to_chunks(kh, h, c)
    vc = _to_chunks(v.astype(F32), h, c)
    log_g = jnp.log1p(-jnp.exp2(-5.0 - jnp.arange(h, dtype=F32)))
    idx = jnp.arange(c, dtype=F32)
    rel = idx[:, None] - idx[None, :]
    dmat = jnp.where(rel >= 0, jnp.exp(log_g[:, None, None] * jnp.maximum(rel, 0.0)), 0.0)
    scores = jnp.einsum('bhntd,bhnsd->bhnts', qc, kc) * dmat[None, :, None]
    o = jnp.einsum('bhnts,bhnsv->bhntv', scores, vc)
    q_dec = qc * jnp.exp(log_g[:, None] * (idx + 1.0))[None, :, None, :, None]
    k_dec = kc * jnp.exp(log_g[:, None] * (c - 1.0 - idx))[None, :, None, :, None]
    chunk_decay = jnp.broadcast_to(jnp.exp(log_g * c)[None, :, None, None], (bsz, h, n, dk))
    o = _from_chunks(o + _inter_chunk(q_dec, k_dec, vc, chunk_decay))
    mu = jnp.mean(o, axis=-1, keepdims=True)
    var = jnp.mean((o - mu) ** 2, axis=-1, keepdims=True)
    o = (o - mu) * lax.rsqrt(var + EPS)
    return o.reshape(bsz, length, h * dv) * jax.nn.silu(gate.astype(F32))


def cross_attn(h, mem_n, wq, wkv, wo):
    bsz, length, d = h.shape
    m = mem_n.shape[1]
    q = (h @ wq).reshape(bsz, length, XA_HEADS, XA_DH)
    k, v = jnp.split(mem_n @ wkv, 2, axis=-1)
    k = k.reshape(bsz, m, XA_HEADS, XA_DH)
    v = v.reshape(bsz, m, XA_HEADS, XA_DH)
    s = jnp.einsum('blhd,bmhd->bhlm', q, k).astype(F32) * XA_DH ** -0.5
    p = jax.nn.softmax(s, axis=-1).astype(h.dtype)
    o = jnp.einsum('bhlm,bmhd->blhd', p, v).reshape(bsz, length, d)
    return o @ wo


def peer(h, wq, keys, u_tab, v_tab):
    bsz, length, d = h.shape
    q = (h @ wq).astype(F32).reshape(bsz, length, PEER_HEADS, 2, PEER_DH)
    s = jnp.einsum('blhpd,hpkd->blhpk', q, keys.astype(F32))
    s_top, i_top = lax.top_k(s, PEER_TOPK)
    cand = (s_top[..., 0, :, None] + s_top[..., 1, None, :]).reshape(bsz, length, PEER_HEADS, PEER_TOPK * PEER_TOPK)
    cand_idx = (i_top[..., 0, :, None] * PEER_NKEYS + i_top[..., 1, None, :]).reshape(bsz, length, PEER_HEADS, PEER_TOPK * PEER_TOPK)
    best, pos = lax.top_k(cand, PEER_TOPK)
    idx = jnp.take_along_axis(cand_idx, pos, axis=-1)
    gates = jax.nn.softmax(best, axis=-1)
    nb = length // PEER_BLOCK
    hb = h.reshape(bsz, nb, PEER_BLOCK, d).swapaxes(0, 1)
    ib = idx.reshape(bsz, nb, PEER_BLOCK, PEER_HEADS, PEER_TOPK).swapaxes(0, 1)
    gb = gates.reshape(bsz, nb, PEER_BLOCK, PEER_HEADS, PEER_TOPK).swapaxes(0, 1)

    def block(args):
        hh, ii, gg = args
        u = u_tab[ii]
        act = jax.nn.gelu(jnp.einsum('bthkd,btd->bthk', u, hh).astype(F32)) * gg
        vv = v_tab[ii]
        return jnp.einsum('bthk,bthkd->btd', act.astype(h.dtype), vv)

    y = lax.map(block, (hb, ib, gb))
    return y.swapaxes(0, 1).reshape(bsz, length, d)


def setup_inputs(seed: int = 0) -> dict:
    key = jax.random.key(seed)
    keys = jax.random.split(key, 48)
    ctr = [0]

    def nk():
        k = keys[ctr[0]]
        ctr[0] += 1
        return k

    def nrm(shape, scale):
        return jax.random.normal(nk(), shape, F32) * scale

    def gain(shape):
        return 1.0 + nrm(shape, 0.02)

    L, D = DEPTH, D_MODEL
    x = nrm((BATCH, SEQ, D), 1.0)
    mem = nrm((BATCH, MEM_LEN, D), 1.0)
    offs = jax.random.randint(nk(), (BATCH, 1), 0, 4096, dtype=jnp.int32)
    positions = offs + jnp.arange(SEQ, dtype=jnp.int32)[None, :]
    n_idx = jnp.arange(S5_STATE, dtype=F32)
    return {
        'x': x,
        'mem': mem,
        'positions': positions,
        'norm_mix': gain((L, D)),
        'w_in': nrm((L, D, IN_COLS), D ** -0.5),
        'b_gate': nrm((L, N_BRANCH * D), 0.01),
        'hgrn_lb': nrm((L, HG_WIDTH), 0.1),
        'hgrn_norm': gain((L, HG_WIDTH)),
        's5_lambda_re': -0.5 + nrm((L, S5_GROUPS, S5_STATE), 0.01),
        's5_lambda_im': jnp.pi * n_idx + nrm((L, S5_GROUPS, S5_STATE), 0.01),
        's5_log_step': jax.random.uniform(nk(), (L, S5_GROUPS), F32, math.log(1e-3), math.log(1e-1)),
        's5_b_re': nrm((L, S5_GROUPS, S5_STATE, S5_GROUP), (2.0 * S5_GROUP) ** -0.5),
        's5_b_im': nrm((L, S5_GROUPS, S5_STATE, S5_GROUP), (2.0 * S5_GROUP) ** -0.5),
        's5_c_re': nrm((L, S5_GROUPS, S5_GROUP, S5_STATE), (2.0 * S5_STATE) ** -0.5),
        's5_c_im': nrm((L, S5_GROUPS, S5_GROUP, S5_STATE), (2.0 * S5_STATE) ** -0.5),
        's5_d': nrm((L, S5_WIDTH), 1.0),
        's5_w_glu': nrm((L, S5_WIDTH, S5_WIDTH), S5_WIDTH ** -0.5),
        's5_b_glu': nrm((L, S5_WIDTH), 0.01),
        'w_branch': nrm((L, MIX_WIDTH, D), HG_WIDTH ** -0.5),
        'w_out': nrm((L, D, D), D ** -0.5),
        'norm_cross': gain((L, D)),
        'norm_mem': gain((L, D)),
        'xa_wq': nrm((L, D, D), D ** -0.5),
        'xa_wkv': nrm((L, D, 2 * D), D ** -0.5),
        'xa_wo': nrm((L, D, D), D ** -0.5),
        'norm_ffn': gain((L, D)),
        'peer_wq': nrm((L, D, PEER_HEADS * PEER_DQ), D ** -0.5),
        'peer_keys': nrm((L, PEER_HEADS, 2, PEER_NKEYS, PEER_DH), PEER_DH ** -0.5),
        'peer_u': nrm((L, PEER_EXPERTS, D), D ** -0.5),
        'peer_v': nrm((L, PEER_EXPERTS, D), 0.25),
        'norm_final': gain((D,)),
    }


def reference(x, mem, positions, norm_mix, w_in, b_gate, hgrn_lb, hgrn_norm, s5_lambda_re, s5_lambda_im,
              s5_log_step, s5_b_re, s5_b_im, s5_c_re, s5_c_im, s5_d, s5_w_glu, s5_b_glu, w_branch, w_out,
              norm_cross, norm_mem, xa_wq, xa_wkv, xa_wo, norm_ffn, peer_wq, peer_keys, peer_u, peer_v,
              norm_final):
    bsz, length, d = x.shape
    sm = jax.nn.softmax(hgrn_lb.astype(F32), axis=0)
    lower_bounds = jnp.cumsum(sm, axis=0) - sm[0]
    split_at = np.cumsum([HG_WIDTH] * 4 + [S5_WIDTH] + [RET_WIDTH] * 4).tolist()
    e0, e1 = HG_WIDTH, HG_WIDTH + S5_WIDTH
    for l in range(DEPTH):
        h = rmsnorm(x, norm_mix[l])
        proj = h @ w_in[l]
        hg_q, hg_f, hg_i, hg_g, s5_u, rt_q, rt_k, rt_v, rt_g, gate_logits = jnp.split(proj, split_at, axis=-1)
        o_hg = hgrn2(hg_q, hg_f, hg_i, hg_g, lower_bounds[l], hgrn_norm[l]).astype(h.dtype)
        o_s5 = s5(s5_u, s5_lambda_re[l], s5_lambda_im[l], s5_log_step[l], s5_b_re[l], s5_b_im[l],
                  s5_c_re[l], s5_c_im[l], s5_d[l], s5_w_glu[l], s5_b_glu[l]).astype(h.dtype)
        o_rt = retention(rt_q, rt_k, rt_v, rt_g, positions).astype(h.dtype)
        gates = jax.nn.sigmoid((gate_logits + b_gate[l]).astype(F32)).reshape(bsz, length, N_BRANCH, d).astype(h.dtype)
        wb = w_branch[l]
        merged = (gates[:, :, 0] * (o_hg @ wb[:e0])
                  + gates[:, :, 1] * (o_s5 @ wb[e0:e1])
                  + gates[:, :, 2] * (o_rt @ wb[e1:]))
        x = x + merged @ w_out[l]
        x = x + cross_attn(rmsnorm(x, norm_cross[l]), rmsnorm(mem, norm_mem[l]), xa_wq[l], xa_wkv[l], xa_wo[l])
        x = x + peer(rmsnorm(x, norm_ffn[l]), peer_wq[l], peer_keys[l], peer_u[l], peer_v[l])
    return rmsnorm(x, norm_final)
```

```python
import functools
import math

import jax
import jax.numpy as jnp
from jax import lax
from jax.experimental import pallas as pl
from jax.experimental.pallas import tpu as pltpu

F32 = jnp.float32
BF16 = jnp.bfloat16

D_MODEL = 1024
HEADS = 4
HEAD_DIM = 128
MIX = HEADS * HEAD_DIM
HG_CHUNK = 16
HG_F_MIN = 1e-6
S5_GROUPS = 32
S5_GROUP = 16
S5_STATE = 64
S5_WIDTH = S5_GROUPS * S5_GROUP
S5_LANES = S5_GROUPS * S5_STATE
RET_CHUNK = 128
ROPE_BASE = 10000.0
IN_COLS = 4 * MIX + S5_WIDTH + 4 * MIX + 3 * D_MODEL
XA_HEADS = 4
XA_DH = D_MODEL // XA_HEADS
PEER_HEADS = 8
PEER_NKEYS = 128
PEER_DH = 128
PEER_TOPK = 16
EPS = 1e-6
NEG_INF = float("-inf")
UNRANKED = 99.0

VMEM_LIMIT = 56 * 1024 * 1024

NT_DIMS = (((1,), (1,)), ((), ()))
TN_DIMS = (((0,), (0,)), ((), ()))


def _params(sem):
    return pltpu.CompilerParams(dimension_semantics=sem, vmem_limit_bytes=VMEM_LIMIT)


def _rms(x, gain):
    return x * lax.rsqrt(jnp.mean(x * x, axis=-1, keepdims=True) + EPS) * gain


def _gelu(x):
    return 0.5 * x * (1.0 + jnp.tanh(math.sqrt(2.0 / math.pi) * (x + 0.044715 * (x * x * x))))


def _sigmoid(x):
    return 1.0 / (1.0 + jnp.exp(-x))


def _silu(x):
    return x * _sigmoid(x)


def _dot(a, b):
    return jnp.dot(a, b, preferred_element_type=F32)


def _dot_nt(a, b):
    return lax.dot_general(a, b, NT_DIMS, preferred_element_type=F32)


def _dot_tn(a, b):
    return lax.dot_general(a, b, TN_DIMS, preferred_element_type=F32)


def _inproj_kernel(x_ref, g_ref, w_ref, o_ref, h_scr):
    @pl.when(pl.program_id(1) == 0)
    def _():
        h_scr[...] = _rms(x_ref[...], g_ref[...]).astype(BF16)

    o_ref[...] = _dot(h_scr[...], w_ref[...])


def _inproj(x2d, gain, w_bf16, tm=1024, tn=1536):
    m, d = x2d.shape
    n = w_bf16.shape[1]
    return pl.pallas_call(
        _inproj_kernel,
        grid=(m // tm, n // tn),
        in_specs=[
            pl.BlockSpec((tm, d), lambda i, j: (i, 0)),
            pl.BlockSpec((1, d), lambda i, j: (0, 0)),
            pl.BlockSpec((d, tn), lambda i, j: (0, j)),
        ],
        out_specs=pl.BlockSpec((tm, tn), lambda i, j: (i, j)),
        out_shape=jax.ShapeDtypeStruct((m, n), F32),
        scratch_shapes=[pltpu.VMEM((tm, d), BF16)],
        compiler_params=_params(("parallel", "arbitrary")),
        name="inproj",
    )(x2d, gain, w_bf16)


def _hgrn_kernel(layer, q_ref, f_ref, i_ref, g_ref, lb_ref, ng_ref, ones_ref, o_ref,
                 st_scr, b_scr, q_scr, k_scr, v_scr, oi_scr):
    tc = q_ref.shape[1]
    c = HG_CHUNK

    @pl.when(pl.program_id(1) == 0)
    def _():
        st_scr[...] = jnp.zeros_like(st_scr)

    lbs = lb_ref[...]
    e = jnp.exp(lbs - jnp.max(lbs, axis=0, keepdims=True))
    sm = e / jnp.sum(e, axis=0, keepdims=True)
    lower = jnp.sum(sm[0:layer + 1], axis=0, keepdims=True) - sm[0:1]

    z = f_ref[0]
    fv = lower + (1.0 - lower) * _sigmoid(z)
    logf = jnp.log(jnp.clip(fv, HG_F_MIN, 1.0))
    kk = (1.0 - lower) * _sigmoid(-z)
    qs = q_ref[0] * HEAD_DIM ** -0.5
    v = i_ref[0]

    rin = lax.broadcasted_iota(jnp.int32, (tc, MIX), 0) & (c - 1)
    b = logf
    d = 1
    while d < c:
        b = b + jnp.where(rin >= d, pltpu.roll(b, d, 0), 0.0)
        d *= 2

    ones2 = ones_ref[...]
    o = jnp.zeros((tc, MIX), F32)
    for off in range(c):
        if off == 0:
            bs, ks, vs = b, kk, v
        else:
            bs, ks, vs = pltpu.roll(b, off, 0), pltpu.roll(kk, off, 0), pltpu.roll(v, off, 0)
        dec = jnp.exp(jnp.minimum(b - bs, 0.0))
        prod = jnp.where(rin >= off, dec * qs * ks, 0.0).astype(BF16)
        att = jnp.concatenate(
            [_dot(prod[:, 0:256], ones2), _dot(prod[:, 256:512], ones2)], axis=1)
        o = o + att * vs

    b_scr[...] = b
    q_scr[...] = qs
    k_scr[...] = kk
    v_scr[...] = v

    def chunk(ci, carry):
        r0 = pl.multiple_of(ci * c, c)
        bc = b_scr[pl.ds(r0, c), :]
        bl = bc[c - 1:c, :]
        qd = (q_scr[pl.ds(r0, c), :] * jnp.exp(bc)).astype(BF16)
        kd = (k_scr[pl.ds(r0, c), :] * jnp.exp(bl - bc)).astype(BF16)
        vc = v_scr[pl.ds(r0, c), :].astype(BF16)
        dec_l = jnp.exp(bl)
        for hh in range(HEADS):
            sl = slice(HEAD_DIM * hh, HEAD_DIM * (hh + 1))
            st = st_scr[hh]
            oi_scr[pl.ds(r0, c), sl] = _dot_nt(qd[:, sl], st.astype(BF16))
            st_scr[hh] = st * dec_l[:, sl] + _dot_tn(vc[:, sl], kd[:, sl])
        return carry

    lax.fori_loop(0, tc // c, chunk, 0)
    o = o + oi_scr[...]

    gate = g_ref[0]
    ng = ng_ref[...]
    outs = []
    for hh in range(HEADS):
        sl = slice(HEAD_DIM * hh, HEAD_DIM * (hh + 1))
        oh = o[:, sl]
        outs.append(oh * lax.rsqrt(jnp.mean(oh * oh, axis=-1, keepdims=True) + EPS) * ng[:, sl])
    o_ref[0] = (jnp.concatenate(outs, axis=1) * _silu(gate)).astype(o_ref.dtype)


def _hgrn(proj3, lb, ng, layer, tc=256):
    bsz, length, _ = proj3.shape
    ones2 = jnp.kron(jnp.eye(2, dtype=F32), jnp.ones((HEAD_DIM, HEAD_DIM), F32)).astype(BF16)
    col = lambda k: pl.BlockSpec((1, tc, MIX), lambda bi, ti, k=k: (bi, ti, k))
    return pl.pallas_call(
        functools.partial(_hgrn_kernel, layer),
        grid=(bsz, length // tc),
        in_specs=[col(0), col(1), col(2), col(3),
                  pl.BlockSpec(lb.shape, lambda bi, ti: (0, 0)),
                  pl.BlockSpec((1, MIX), lambda bi, ti: (0, 0)),
                  pl.BlockSpec((256, 256), lambda bi, ti: (0, 0))],
        out_specs=pl.BlockSpec((1, tc, MIX), lambda bi, ti: (bi, ti, 0)),
        out_shape=jax.ShapeDtypeStruct((bsz, length, MIX), BF16),
        scratch_shapes=[pltpu.VMEM((HEADS, HEAD_DIM, HEAD_DIM), F32)]
        + [pltpu.VMEM((tc, MIX), F32)] * 5,
        compiler_params=_params(("parallel", "arbitrary")),
        name="hgrn2",
    )(proj3, proj3, proj3, proj3, lb, ng, ones2)


def _s5_prep_kernel(lr_ref, li_ref, ls_ref, bre_ref, bim_ref, bbre_ref, bbim_ref, tab_ref):
    lr = jnp.minimum(lr_ref[...], -1e-4)
    li = li_ref[...]
    dt = jnp.exp(ls_ref[...])
    mag = jnp.exp(lr * dt)
    a_re = mag * jnp.cos(li * dt)
    a_im = mag * jnp.sin(li * dt)
    den = lr * lr + li * li
    z_re = ((a_re - 1.0) * lr + a_im * li) / den
    z_im = (a_im * lr - (a_re - 1.0) * li) / den
    bre = bre_ref[...]
    bim = bim_ref[...]
    bbre_ref[...] = z_re * bre - z_im * bim
    bbim_ref[...] = z_re * bim + z_im * bre
    row = lax.broadcasted_iota(jnp.int32, (8, a_re.shape[1]), 0)
    p_re = jnp.broadcast_to(a_re, row.shape)
    p_im = jnp.broadcast_to(a_im, row.shape)
    c_re, c_im = a_re, a_im
    powers = [(a_re, a_im)]
    for r in range(1, 8):
        c_re, c_im = c_re * a_re - c_im * a_im, c_re * a_im + c_im * a_re
        powers.append((c_re, c_im))
        p_re = jnp.where(row == r, c_re, p_re)
        p_im = jnp.where(row == r, c_im, p_im)
    for slot, d in enumerate((1, 2, 4)):
        tab_ref[2 * slot] = jnp.where(row >= d, powers[d - 1][0], 0.0)
        tab_ref[2 * slot + 1] = jnp.where(row >= d, powers[d - 1][1], 0.0)
    tab_ref[6] = p_re
    tab_ref[7] = p_im


def _s5_prep(lam_re, lam_im, log_step, b_re, b_im):
    lanes = S5_LANES
    row = lambda a: a.astype(F32).reshape(1, lanes)
    ls = jnp.repeat(log_step.astype(F32), S5_STATE).reshape(1, lanes)
    bt = lambda a: a.astype(F32).reshape(lanes, S5_GROUP).T
    full = lambda shape: pl.BlockSpec(shape, lambda: tuple(0 for _ in shape))
    return pl.pallas_call(
        _s5_prep_kernel,
        in_specs=[full((1, lanes))] * 3 + [full((S5_GROUP, lanes))] * 2,
        out_specs=[full((S5_GROUP, lanes))] * 2 + [full((8, 8, lanes))],
        out_shape=[jax.ShapeDtypeStruct((S5_GROUP, lanes), F32)] * 2
        + [jax.ShapeDtypeStruct((8, 8, lanes), F32)],
        name="s5_prep",
    )(row(lam_re), row(lam_im), ls, bt(b_re), bt(b_im))


def _s5_kernel(u_ref, bbd_ref, tab_ref, cre_ref, cim_ref, d_ref, wg_ref, bg_ref, o_ref,
               xr_scr, xi_scr, car_scr):
    tt = u_ref.shape[1]
    lanes = S5_LANES
    lw = 512

    @pl.when(pl.program_id(1) == 0)
    def _():
        car_scr[...] = jnp.zeros_like(car_scr)

    u = u_ref[0]
    bu = _dot(u.astype(BF16), bbd_ref[...])
    xr_scr[...] = bu[:, :lanes]
    xi_scr[...] = bu[:, lanes:]

    for lc in range(lanes // lw):
        sl = slice(lw * lc, lw * (lc + 1))
        steps = [(d, tab_ref[2 * s][:, sl], tab_ref[2 * s + 1][:, sl]) for s, d in enumerate((1, 2, 4))]
        p_re = tab_ref[6][:, sl]
        p_im = tab_ref[7][:, sl]

        def body(k, carry, sl=sl, steps=steps, p_re=p_re, p_im=p_im):
            c_re, c_im = carry
            r0 = pl.multiple_of(k * 8, 8)
            xr = xr_scr[pl.ds(r0, 8), sl]
            xi = xi_scr[pl.ds(r0, 8), sl]
            for d, a_re, a_im in steps:
                sr = pltpu.roll(xr, d, 0)
                si = pltpu.roll(xi, d, 0)
                xr, xi = xr + a_re * sr - a_im * si, xi + a_re * si + a_im * sr
            xr, xi = xr + p_re * c_re - p_im * c_im, xi + p_re * c_im + p_im * c_re
            xr_scr[pl.ds(r0, 8), sl] = xr
            xi_scr[pl.ds(r0, 8), sl] = xi
            return xr[7:8, :], xi[7:8, :]

        c_re, c_im = lax.fori_loop(0, tt // 8, body, (car_scr[0:1, sl], car_scr[1:2, sl]))
        car_scr[0:1, sl] = c_re
        car_scr[1:2, sl] = c_im

    y = (_dot(xr_scr[...].astype(BF16), cre_ref[...]) - _dot(xi_scr[...].astype(BF16), cim_ref[...])
         + d_ref[...] * u)
    y = _gelu(y)
    o_ref[0] = (y * _sigmoid(_dot(y.astype(BF16), wg_ref[...]) + bg_ref[...])).astype(o_ref.dtype)


def _s5(proj3, bbd, tab, cre, cim, dskip, wg, bg, tt=256):
    bsz, length, _ = proj3.shape
    lanes = S5_LANES
    const = lambda shape: pl.BlockSpec(shape, lambda bi, ti: tuple(0 for _ in shape))
    return pl.pallas_call(
        _s5_kernel,
        grid=(bsz, length // tt),
        in_specs=[pl.BlockSpec((1, tt, S5_WIDTH), lambda bi, ti: (bi, ti, 4)),
                  const((S5_WIDTH, 2 * lanes)), const((8, 8, lanes)),
                  const((lanes, S5_WIDTH)), const((lanes, S5_WIDTH)),
                  const((1, S5_WIDTH)), const((S5_WIDTH, S5_WIDTH)), const((1, S5_WIDTH))],
        out_specs=pl.BlockSpec((1, tt, S5_WIDTH), lambda bi, ti: (bi, ti, 0)),
        out_shape=jax.ShapeDtypeStruct((bsz, length, S5_WIDTH), BF16),
        scratch_shapes=[pltpu.VMEM((tt, lanes), F32), pltpu.VMEM((tt, lanes), F32),
                        pltpu.VMEM((8, lanes), F32)],
        compiler_params=_params(("parallel", "arbitrary")),
        name="s5",
    )(proj3, bbd, tab, cre, cim, dskip, wg, bg)


def _block_diag_in(bb_t):
    g, c, p = S5_GROUPS, S5_GROUP, S5_STATE
    eye = jnp.eye(g, dtype=F32)
    full = eye[:, None, :, None] * bb_t.reshape(c, g, p).transpose(1, 0, 2)[:, :, None, :]
    return full.reshape(g * c, g * p)


def _block_diag_out(cmat):
    g, c, p = S5_GROUPS, S5_GROUP, S5_STATE
    eye = jnp.eye(g, dtype=F32)
    full = eye[:, None, :, None] * cmat.astype(F32).transpose(0, 2, 1)[:, :, None, :]
    return full.reshape(g * p, g * c)


def _rope_kernel(pos_ref, cos_ref, sin_ref):
    pos = pos_ref[0].astype(F32)
    half = HEAD_DIM // 2
    lane = lax.broadcasted_iota(jnp.int32, (1, HEAD_DIM), 1)
    j = (lane & (half - 1)).astype(F32)
    inv = jnp.exp(j * (-math.log(ROPE_BASE) / half))
    ang = pos * inv
    cos_ref[0] = jnp.cos(ang)
    sin_ref[0] = jnp.where(lane < half, -1.0, 1.0) * jnp.sin(ang)


def _rope_tables(positions, tr=1024):
    bsz, length = positions.shape
    tr = min(tr, length)
    spec_o = pl.BlockSpec((1, tr, HEAD_DIM), lambda bi, ti: (bi, ti, 0))
    return pl.pallas_call(
        _rope_kernel,
        grid=(bsz, length // tr),
        in_specs=[pl.BlockSpec((1, tr, 1), lambda bi, ti: (bi, ti, 0))],
        out_specs=[spec_o, spec_o],
        out_shape=[jax.ShapeDtypeStruct((bsz, length, HEAD_DIM), F32)] * 2,
        compiler_params=_params(("parallel", "parallel")),
        name="rope_tables",
    )(positions.reshape(bsz, length, 1))


def _ret_kernel(q_ref, k_ref, v_ref, g_ref, cos_ref, sin_ref, o_ref, s_scr):
    c = RET_CHUNK

    @pl.when(pl.program_id(1) == 0)
    def _():
        s_scr[...] = jnp.zeros_like(s_scr)

    cosf = cos_ref[0]
    sinf = sin_ref[0]
    rel = (lax.broadcasted_iota(jnp.int32, (c, c), 0) - lax.broadcasted_iota(jnp.int32, (c, c), 1)).astype(F32)
    idx = lax.broadcasted_iota(jnp.int32, (c, 1), 0).astype(F32)
    qa, ka, va, ga = q_ref[0], k_ref[0], v_ref[0], g_ref[0]
    outs = []
    for hh in range(HEADS):
        sl = slice(HEAD_DIM * hh, HEAD_DIM * (hh + 1))
        lg = math.log1p(-2.0 ** (-5.0 - hh))
        q = qa[:, sl]
        k = ka[:, sl]
        vb = va[:, sl].astype(BF16)
        qh = q * cosf + pltpu.roll(q, HEAD_DIM // 2, 1) * sinf
        kh = (k * cosf + pltpu.roll(k, HEAD_DIM // 2, 1) * sinf) * HEAD_DIM ** -0.5
        dmat = jnp.where(rel >= 0.0, jnp.exp(lg * jnp.maximum(rel, 0.0)), 0.0)
        scores = _dot_nt(qh.astype(BF16), kh.astype(BF16)) * dmat
        o = _dot(scores.astype(BF16), vb)
        q_dec = qh * jnp.exp(lg * (idx + 1.0))
        k_dec = kh * jnp.exp(lg * (c - 1.0 - idx))
        st = s_scr[hh]
        o = o + _dot(q_dec.astype(BF16), st.astype(BF16))
        s_scr[hh] = math.exp(lg * c) * st + _dot_tn(k_dec.astype(BF16), vb)
        mu = jnp.mean(o, axis=-1, keepdims=True)
        var = jnp.mean((o - mu) ** 2, axis=-1, keepdims=True)
        outs.append((o - mu) * lax.rsqrt(var + EPS))
    o_ref[0] = (jnp.concatenate(outs, axis=1) * _silu(ga)).astype(o_ref.dtype)


def _retention(proj3, cos_t, sin_t):
    bsz, length, _ = proj3.shape
    c = RET_CHUNK
    col = lambda k: pl.BlockSpec((1, c, MIX), lambda bi, ti, k=k: (bi, ti, k))
    tab = pl.BlockSpec((1, c, HEAD_DIM), lambda bi, ti: (bi, ti, 0))
    return pl.pallas_call(
        _ret_kernel,
        grid=(bsz, length // c),
        in_specs=[col(5), col(6), col(7), col(8), tab, tab],
        out_specs=pl.BlockSpec((1, c, MIX), lambda bi, ti: (bi, ti, 0)),
        out_shape=jax.ShapeDtypeStruct((bsz, length, MIX), BF16),
        scratch_shapes=[pltpu.VMEM((HEADS, HEAD_DIM, HEAD_DIM), F32)],
        compiler_params=_params(("parallel", "arbitrary")),
        name="retention",
    )(proj3, proj3, proj3, proj3, cos_t, sin_t)


def _memkv_kernel(m_ref, g_ref, w_ref, k_ref, v_ref):
    kv = _dot(_rms(m_ref[0], g_ref[...]).astype(BF16), w_ref[...])
    k_ref[0] = kv[:, :D_MODEL].astype(BF16)
    v_ref[0] = kv[:, D_MODEL:].astype(BF16)


def _memkv(mem, gain, wkv_bf16):
    bsz, m, d = mem.shape
    spec_o = pl.BlockSpec((1, m, d), lambda bi: (bi, 0, 0))
    return pl.pallas_call(
        _memkv_kernel,
        grid=(bsz,),
        in_specs=[pl.BlockSpec((1, m, d), lambda bi: (bi, 0, 0)),
                  pl.BlockSpec((1, d), lambda bi: (0, 0)),
                  pl.BlockSpec((d, 2 * d), lambda bi: (0, 0))],
        out_specs=[spec_o, spec_o],
        out_shape=[jax.ShapeDtypeStruct((bsz, m, d), BF16)] * 2,
        compiler_params=_params(("parallel",)),
        name="mem_kv",
    )(mem, gain, wkv_bf16)


def _mix_kernel(x_ref, ga_ref, gb_ref, bg_ref, hg_ref, s5_ref, rt_ref, wb_ref, wout_ref,
                nc_ref, k_ref, v_ref, wq_ref, wo_ref, o_ref):
    d = D_MODEL
    logits = jnp.concatenate([ga_ref[...], gb_ref[...]], axis=1) + bg_ref[...]
    gates = _sigmoid(logits)
    merged = (gates[:, 0:d] * _dot(hg_ref[...], wb_ref[0:MIX, :])
              + gates[:, d:2 * d] * _dot(s5_ref[...], wb_ref[MIX:MIX + S5_WIDTH, :])
              + gates[:, 2 * d:3 * d] * _dot(rt_ref[...], wb_ref[MIX + S5_WIDTH:, :]))
    x1 = x_ref[...] + _dot(merged.astype(BF16), wout_ref[...])

    hc = _rms(x1, nc_ref[...]).astype(BF16)
    q = _dot(hc, wq_ref[...])
    kk = k_ref[0]
    vv = v_ref[0]
    outs = []
    for hh in range(XA_HEADS):
        sl = slice(XA_DH * hh, XA_DH * (hh + 1))
        s = _dot_nt(q[:, sl].astype(BF16), kk[:, sl]) * XA_DH ** -0.5
        p = jnp.exp(s - jnp.max(s, axis=-1, keepdims=True))
        p = p / jnp.sum(p, axis=-1, keepdims=True)
        outs.append(_dot(p.astype(BF16), vv[:, sl]))
    attn = jnp.concatenate(outs, axis=1)
    o_ref[...] = x1 + _dot(attn.astype(BF16), wo_ref[...])


def _mix(x2d, proj, bgate, o_hg, o_s5, o_rt, wb, wout, ncross, kmem, vmem, wq, wo, length, tm=512):
    m, d = x2d.shape
    per_batch = length // tm
    tok = lambda w: pl.BlockSpec((tm, w), lambda i: (i, 0))
    const = lambda shape: pl.BlockSpec(shape, lambda i: tuple(0 for _ in shape))
    gw = 1536
    mem_spec = pl.BlockSpec((1,) + kmem.shape[1:], lambda i: (i // per_batch, 0, 0))
    return pl.pallas_call(
        _mix_kernel,
        grid=(m // tm,),
        in_specs=[tok(d),
                  pl.BlockSpec((tm, gw), lambda i: (i, 3)), pl.BlockSpec((tm, gw), lambda i: (i, 4)),
                  const((1, 3 * d)), tok(MIX), tok(S5_WIDTH), tok(MIX),
                  const(wb.shape), const((d, d)), const((1, d)), mem_spec, mem_spec,
                  const((d, d)), const((d, d))],
        out_specs=tok(d),
        out_shape=jax.ShapeDtypeStruct((m, d), F32),
        compiler_params=_params(("parallel",)),
        name="merge_xattn",
    )(x2d, proj, proj, bgate, o_hg, o_s5, o_rt, wb, wout, ncross, kmem, vmem, wq, wo)


def _top_extract(s):
    t = s.shape[1]
    row16 = lax.broadcasted_iota(jnp.int32, (PEER_TOPK, t), 0)
    vals = jnp.zeros((PEER_TOPK, t), F32)
    rank = jnp.full(s.shape, UNRANKED, F32)
    w = s
    for r in range(PEER_TOPK):
        m = jnp.max(w, axis=0, keepdims=True)
        hit = w == m
        rank = jnp.where(hit, float(r + 1), rank)
        w = jnp.where(hit, NEG_INF, w)
        vals = jnp.where(row16 == r, m, vals)
    return vals, rank


def _route_kernel(x_ref, g_ref, wqt_ref, keys_ref, h_ref, rank2_ref, e2_ref, n1_ref, e1_ref):
    t = x_ref.shape[0]
    k = PEER_TOPK
    hb = _rms(x_ref[...], g_ref[...]).astype(BF16)
    h_ref[...] = hb
    qt = _dot_nt(wqt_ref[...], hb).astype(BF16)
    row16 = lax.broadcasted_iota(jnp.int32, (k, t), 0)
    row8 = lax.broadcasted_iota(jnp.int32, (8, t), 0)
    for hd in range(PEER_HEADS):
        base = hd * 2 * PEER_DH
        s1 = _dot(keys_ref[2 * hd], qt[base:base + PEER_DH])
        s2 = _dot(keys_ref[2 * hd + 1], qt[base + PEER_DH:base + 2 * PEER_DH])
        a, rank1 = _top_extract(s1)
        b, rank2 = _top_extract(s2)
        cands = []
        cands.append((a + b[0:1], None))
        for j in range(2, 9):
            lim = k // j
            cands.append((jnp.where(row8 < lim, a[0:8] + b[j - 1:j], NEG_INF), None))
        tail = a[0:1] + b[8:16]
        top = a[0:1] + b[0:1]
        work = [cnd for cnd, _ in cands] + [tail]
        thr = None
        for r in range(k):
            m = work[0].max(axis=0, keepdims=True)
            for wk in work[1:]:
                m = jnp.maximum(m, wk.max(axis=0, keepdims=True))
            work = [jnp.where(wk == m, NEG_INF, wk) for wk in work]
            thr = m
        n_of_i = (cands[0][0] >= thr).astype(F32)
        zsum = jnp.sum(jnp.where(cands[0][0] >= thr, jnp.exp(cands[0][0] - top), 0.0), axis=0, keepdims=True)
        low = jnp.zeros((8, t), F32)
        for cnd, _ in cands[1:]:
            sel = cnd >= thr
            low = low + sel.astype(F32)
            zsum = zsum + jnp.sum(jnp.where(sel, jnp.exp(cnd - top), 0.0), axis=0, keepdims=True)
        sel = tail >= thr
        cnt_tail = jnp.sum(sel.astype(F32), axis=0, keepdims=True)
        zsum = zsum + jnp.sum(jnp.where(sel, jnp.exp(tail - top), 0.0), axis=0, keepdims=True)
        n_of_i = n_of_i + jnp.concatenate([low, jnp.zeros((8, t), F32)], axis=0)
        n_of_i = n_of_i + jnp.where(row16 == 0, cnt_tail, 0.0)
        n1 = jnp.zeros(s1.shape, F32)
        for i in range(k):
            n1 = jnp.where(rank1 == float(i + 1), n_of_i[i:i + 1], n1)
        rows = slice(hd * PEER_NKEYS, (hd + 1) * PEER_NKEYS)
        rank2_ref[rows, :] = rank2.astype(BF16)
        e2_ref[rows, :] = (jnp.exp(s2 - b[0:1]) / zsum).astype(BF16)
        n1_ref[rows, :] = n1.astype(BF16)
        e1_ref[rows, :] = jnp.exp(s1 - a[0:1]).astype(BF16)


def _route(x2d, gain, wqt, keys2, tt=256):
    m, d = x2d.shape
    rows = PEER_HEADS * PEER_NKEYS
    tab = pl.BlockSpec((rows, tt), lambda i: (0, i))
    return pl.pallas_call(
        _route_kernel,
        grid=(m // tt,),
        in_specs=[pl.BlockSpec((tt, d), lambda i: (i, 0)),
                  pl.BlockSpec((1, d), lambda i: (0, 0)),
                  pl.BlockSpec(wqt.shape, lambda i: (0, 0)),
                  pl.BlockSpec(keys2.shape, lambda i: (0, 0, 0))],
        out_specs=[pl.BlockSpec((tt, d), lambda i: (i, 0)), tab, tab, tab, tab],
        out_shape=[jax.ShapeDtypeStruct((m, d), BF16)] + [jax.ShapeDtypeStruct((rows, m), BF16)] * 4,
        compiler_params=_params(("parallel",)),
        name="peer_route",
    )(x2d, gain, wqt, keys2)


def _peer_kernel(final, x_ref, h_ref, rank2_ref, e2_ref, n1_ref, e1_ref, u_ref, vt_ref, gf_ref,
                 o_ref, acc_scr):
    j = pl.program_id(1)
    eb = u_ref.shape[0]
    nk = PEER_NKEYS
    chunk = 512
    per_chunk = chunk // nk

    @pl.when(j == 0)
    def _():
        acc_scr[...] = jnp.zeros_like(acc_scr)

    h = h_ref[...]
    i1_base = pl.multiple_of(j * (eb // nk), eb // nk)
    n1_blk = [n1_ref[pl.ds(hd * nk + i1_base, eb // nk), :] for hd in range(PEER_HEADS)]
    e1_blk = [e1_ref[pl.ds(hd * nk + i1_base, eb // nk), :] for hd in range(PEER_HEADS)]
    for ck in range(eb // chunk):
        z = _dot_nt(u_ref[ck * chunk:(ck + 1) * chunk, :], h)
        act = _gelu(z)
        parts = []
        for sub in range(per_chunk):
            il = ck * per_chunk + sub
            gate = None
            for hd in range(PEER_HEADS):
                rows = slice(hd * nk, (hd + 1) * nk)
                sel = jnp.where(rank2_ref[rows, :] <= n1_blk[hd][il:il + 1, :], e2_ref[rows, :], 0.0)
                term = sel * e1_blk[hd][il:il + 1, :]
                gate = term if gate is None else gate + term
            parts.append(act[sub * nk:(sub + 1) * nk, :].astype(BF16) * gate)
        acc_scr[...] += _dot(vt_ref[:, ck * chunk:(ck + 1) * chunk], jnp.concatenate(parts, axis=0))

    @pl.when(j == pl.num_programs(1) - 1)
    def _():
        y = x_ref[...] + acc_scr[...].T
        if final:
            y = _rms(y, gf_ref[...])
        o_ref[...] = y


def _peer(x2d, h, rank2, e2, n1, e1, u_bf16, vt_bf16, gfinal, final, tt=512, eb=2048):
    m, d = x2d.shape
    ne = u_bf16.shape[0]
    rows = PEER_HEADS * PEER_NKEYS
    tab = pl.BlockSpec((rows, tt), lambda i, j: (0, i))
    return pl.pallas_call(
        functools.partial(_peer_kernel, final),
        grid=(m // tt, ne // eb),
        in_specs=[pl.BlockSpec((tt, d), lambda i, j: (i, 0)),
                  pl.BlockSpec((tt, d), lambda i, j: (i, 0)),
                  tab, tab, tab, tab,
                  pl.BlockSpec((eb, d), lambda i, j: (j, 0)),
                  pl.BlockSpec((d, eb), lambda i, j: (0, j)),
                  pl.BlockSpec((1, d), lambda i, j: (0, 0))],
        out_specs=pl.BlockSpec((tt, d), lambda i, j: (i, 0)),
        out_shape=jax.ShapeDtypeStruct((m, d), F32),
        scratch_shapes=[pltpu.VMEM((d, tt), F32)],
        compiler_params=_params(("parallel", "arbitrary")),
        name="peer_experts",
    )(x2d, h, rank2, e2, n1, e1, u_bf16, vt_bf16, gfinal)


def kernel(x, mem, positions, norm_mix, w_in, b_gate, hgrn_lb, hgrn_norm, s5_lambda_re, s5_lambda_im,
           s5_log_step, s5_b_re, s5_b_im, s5_c_re, s5_c_im, s5_d, s5_w_glu, s5_b_glu, w_branch, w_out,
           norm_cross, norm_mem, xa_wq, xa_wkv, xa_wo, norm_ffn, peer_wq, peer_keys, peer_u, peer_v,
           norm_final):
    bsz, length, d = x.shape
    depth = w_in.shape[0]
    row = lambda a: a.astype(F32).reshape(1, -1)
    x2d = x.astype(F32).reshape(bsz * length, d)
    cos_t, sin_t = _rope_tables(positions)
    lb = hgrn_lb.astype(F32)
    for l in range(depth):
        proj = _inproj(x2d, row(norm_mix[l]), w_in[l].astype(BF16))
        proj3 = proj.reshape(bsz, length, IN_COLS)
        o_hg = _hgrn(proj3, lb, row(hgrn_norm[l]), l)
        bb_re, bb_im, tab = _s5_prep(s5_lambda_re[l], s5_lambda_im[l], s5_log_step[l], s5_b_re[l], s5_b_im[l])
        bbd = jnp.concatenate([_block_diag_in(bb_re), _block_diag_in(bb_im)], axis=1).astype(BF16)
        o_s5 = _s5(proj3, bbd, tab, _block_diag_out(s5_c_re[l]).astype(BF16),
                   _block_diag_out(s5_c_im[l]).astype(BF16), row(s5_d[l]),
                   s5_w_glu[l].astype(BF16), row(s5_b_glu[l]))
        o_rt = _retention(proj3, cos_t, sin_t)
        kmem, vmem = _memkv(mem.astype(F32), row(norm_mem[l]), xa_wkv[l].astype(BF16))
        x2d = _mix(x2d, proj, row(b_gate[l]), o_hg.reshape(-1, MIX), o_s5.reshape(-1, S5_WIDTH),
                   o_rt.reshape(-1, MIX), w_branch[l].astype(BF16), w_out[l].astype(BF16),
                   row(norm_cross[l]), kmem, vmem, xa_wq[l].astype(BF16), xa_wo[l].astype(BF16), length)
        keys2 = peer_keys[l].astype(BF16).reshape(PEER_HEADS * 2, PEER_NKEYS, PEER_DH)
        h, rank2, e2, n1, e1 = _route(x2d, row(norm_ffn[l]), peer_wq[l].astype(BF16).T, keys2)
        x2d = _peer(x2d, h, rank2, e2, n1, e1, peer_u[l].astype(BF16), peer_v[l].astype(BF16).T,
                    row(norm_final), final=(l == depth - 1))
    return x2d.reshape(bsz, length, d)
```

```python
import functools
import math

import jax
import jax.numpy as jnp
from jax import lax
from jax.experimental import pallas as pl
from jax.experimental.pallas import tpu as pltpu

F32 = jnp.float32
BF16 = jnp.bfloat16

D_MODEL = 1024
HEADS = 4
HEAD_DIM = 128
MIX = HEADS * HEAD_DIM
HG_CHUNK = 16
HG_F_MIN = 1e-6
S5_GROUPS = 32
S5_GROUP = 16
S5_STATE = 64
S5_WIDTH = S5_GROUPS * S5_GROUP
S5_LANES = S5_GROUPS * S5_STATE
RET_CHUNK = 128
ROPE_BASE = 10000.0
IN_COLS = 4 * MIX + S5_WIDTH + 4 * MIX + 3 * D_MODEL
XA_HEADS = 4
XA_DH = D_MODEL // XA_HEADS
PEER_HEADS = 8
PEER_NKEYS = 128
PEER_DH = 128
PEER_TOPK = 16
EPS = 1e-6
NEG_INF = float("-inf")
UNRANKED = 99.0
BF16_ROWS = 16

VMEM_LIMIT = 56 * 1024 * 1024

NT_DIMS = (((1,), (1,)), ((), ()))
TN_DIMS = (((0,), (0,)), ((), ()))


def _params(sem):
    return pltpu.CompilerParams(dimension_semantics=sem, vmem_limit_bytes=VMEM_LIMIT)


def _rms(x, gain):
    return x * lax.rsqrt(jnp.mean(x * x, axis=-1, keepdims=True) + EPS) * gain


def _gelu(x):
    return 0.5 * x * (1.0 + jnp.tanh(math.sqrt(2.0 / math.pi) * (x + 0.044715 * (x * x * x))))


def _gelu2(x):
    c = math.sqrt(2.0 / math.pi)
    return x + x * jnp.tanh(x * (c + (c * 0.044715) * (x * x)))


def _sigmoid(x):
    return 1.0 / (1.0 + jnp.exp(-x))


def _silu(x):
    return x * _sigmoid(x)


def _dot(a, b):
    return jnp.dot(a, b, preferred_element_type=F32)


def _dot_nt(a, b):
    return lax.dot_general(a, b, NT_DIMS, preferred_element_type=F32)


def _dot_tn(a, b):
    return lax.dot_general(a, b, TN_DIMS, preferred_element_type=F32)


def _inproj_kernel(x_ref, g_ref, w_ref, o_ref, h_scr):
    @pl.when(pl.program_id(1) == 0)
    def _():
        h_scr[...] = _rms(x_ref[...], g_ref[...]).astype(BF16)

    o_ref[...] = _dot(h_scr[...], w_ref[...])


def _inproj(x2d, gain, w_bf16, tm=1024, tn=1536):
    m, d = x2d.shape
    n = w_bf16.shape[1]
    return pl.pallas_call(
        _inproj_kernel,
        grid=(m // tm, n // tn),
        in_specs=[
            pl.BlockSpec((tm, d), lambda i, j: (i, 0)),
            pl.BlockSpec((1, d), lambda i, j: (0, 0)),
            pl.BlockSpec((d, tn), lambda i, j: (0, j)),
        ],
        out_specs=pl.BlockSpec((tm, tn), lambda i, j: (i, j)),
        out_shape=jax.ShapeDtypeStruct((m, n), F32),
        scratch_shapes=[pltpu.VMEM((tm, d), BF16)],
        compiler_params=_params(("parallel", "arbitrary")),
        name="inproj",
    )(x2d, gain, w_bf16)


def _hgrn_kernel(layer, q_ref, f_ref, i_ref, g_ref, lb_ref, ng_ref, ones_ref, o_ref,
                 st_scr, b_scr, q_scr, k_scr, v_scr, oi_scr):
    tc = q_ref.shape[1]
    c = HG_CHUNK

    @pl.when(pl.program_id(1) == 0)
    def _():
        st_scr[...] = jnp.zeros_like(st_scr)

    lbs = lb_ref[...]
    e = jnp.exp(lbs - jnp.max(lbs, axis=0, keepdims=True))
    sm = e / jnp.sum(e, axis=0, keepdims=True)
    lower = jnp.sum(sm[0:layer + 1], axis=0, keepdims=True) - sm[0:1]

    z = f_ref[0]
    fv = lower + (1.0 - lower) * _sigmoid(z)
    logf = jnp.log(jnp.clip(fv, HG_F_MIN, 1.0))
    kk = (1.0 - lower) * _sigmoid(-z)
    qs = q_ref[0] * HEAD_DIM ** -0.5
    v = i_ref[0]

    rin = lax.broadcasted_iota(jnp.int32, (tc, MIX), 0) & (c - 1)
    b = logf
    d = 1
    while d < c:
        b = b + jnp.where(rin >= d, pltpu.roll(b, d, 0), 0.0)
        d *= 2

    ones2 = ones_ref[...]
    o = jnp.zeros((tc, MIX), F32)
    for off in range(c):
        if off == 0:
            bs, ks, vs = b, kk, v
        else:
            bs, ks, vs = pltpu.roll(b, off, 0), pltpu.roll(kk, off, 0), pltpu.roll(v, off, 0)
        dec = jnp.exp(jnp.minimum(b - bs, 0.0))
        prod = jnp.where(rin >= off, dec * qs * ks, 0.0).astype(BF16)
        att = jnp.concatenate(
            [_dot(prod[:, 0:256], ones2), _dot(prod[:, 256:512], ones2)], axis=1)
        o = o + att * vs

    b_scr[...] = b
    q_scr[...] = qs
    k_scr[...] = kk
    v_scr[...] = v

    def chunk(ci, carry):
        r0 = pl.multiple_of(ci * c, c)
        bc = b_scr[pl.ds(r0, c), :]
        bl = bc[c - 1:c, :]
        qd = (q_scr[pl.ds(r0, c), :] * jnp.exp(bc)).astype(BF16)
        kd = (k_scr[pl.ds(r0, c), :] * jnp.exp(bl - bc)).astype(BF16)
        vc = v_scr[pl.ds(r0, c), :].astype(BF16)
        dec_l = jnp.exp(bl)
        for hh in range(HEADS):
            sl = slice(HEAD_DIM * hh, HEAD_DIM * (hh + 1))
            st = st_scr[hh]
            oi_scr[pl.ds(r0, c), sl] = _dot_nt(qd[:, sl], st.astype(BF16))
            st_scr[hh] = st * dec_l[:, sl] + _dot_tn(vc[:, sl], kd[:, sl])
        return carry

    lax.fori_loop(0, tc // c, chunk, 0)
    o = o + oi_scr[...]

    gate = g_ref[0]
    ng = ng_ref[...]
    outs = []
    for hh in range(HEADS):
        sl = slice(HEAD_DIM * hh, HEAD_DIM * (hh + 1))
        oh = o[:, sl]
        outs.append(oh * lax.rsqrt(jnp.mean(oh * oh, axis=-1, keepdims=True) + EPS) * ng[:, sl])
    o_ref[0] = (jnp.concatenate(outs, axis=1) * _silu(gate)).astype(o_ref.dtype)


def _hgrn(proj3, lb, ng, layer, tc=256):
    bsz, length, _ = proj3.shape
    ones2 = jnp.kron(jnp.eye(2, dtype=F32), jnp.ones((HEAD_DIM, HEAD_DIM), F32)).astype(BF16)
    col = lambda k: pl.BlockSpec((1, tc, MIX), lambda bi, ti, k=k: (bi, ti, k))
    return pl.pallas_call(
        functools.partial(_hgrn_kernel, layer),
        grid=(bsz, length // tc),
        in_specs=[col(0), col(1), col(2), col(3),
                  pl.BlockSpec(lb.shape, lambda bi, ti: (0, 0)),
                  pl.BlockSpec((1, MIX), lambda bi, ti: (0, 0)),
                  pl.BlockSpec((256, 256), lambda bi, ti: (0, 0))],
        out_specs=pl.BlockSpec((1, tc, MIX), lambda bi, ti: (bi, ti, 0)),
        out_shape=jax.ShapeDtypeStruct((bsz, length, MIX), BF16),
        scratch_shapes=[pltpu.VMEM((HEADS, HEAD_DIM, HEAD_DIM), F32)]
        + [pltpu.VMEM((tc, MIX), F32)] * 5,
        compiler_params=_params(("parallel", "arbitrary")),
        name="hgrn2",
    )(proj3, proj3, proj3, proj3, lb, ng, ones2)


def _s5_prep_kernel(lr_ref, li_ref, ls_ref, bre_ref, bim_ref, bbre_ref, bbim_ref, tab_ref):
    lr = jnp.minimum(lr_ref[...], -1e-4)
    li = li_ref[...]
    dt = jnp.exp(ls_ref[...])
    mag = jnp.exp(lr * dt)
    a_re = mag * jnp.cos(li * dt)
    a_im = mag * jnp.sin(li * dt)
    den = lr * lr + li * li
    z_re = ((a_re - 1.0) * lr + a_im * li) / den
    z_im = (a_im * lr - (a_re - 1.0) * li) / den
    bre = bre_ref[...]
    bim = bim_ref[...]
    bbre_ref[...] = z_re * bre - z_im * bim
    bbim_ref[...] = z_re * bim + z_im * bre
    row = lax.broadcasted_iota(jnp.int32, (8, a_re.shape[1]), 0)
    p_re = jnp.broadcast_to(a_re, row.shape)
    p_im = jnp.broadcast_to(a_im, row.shape)
    c_re, c_im = a_re, a_im
    powers = [(a_re, a_im)]
    for r in range(1, 8):
        c_re, c_im = c_re * a_re - c_im * a_im, c_re * a_im + c_im * a_re
        powers.append((c_re, c_im))
        p_re = jnp.where(row == r, c_re, p_re)
        p_im = jnp.where(row == r, c_im, p_im)
    for slot, d in enumerate((1, 2, 4)):
        tab_ref[2 * slot] = jnp.where(row >= d, powers[d - 1][0], 0.0)
        tab_ref[2 * slot + 1] = jnp.where(row >= d, powers[d - 1][1], 0.0)
    tab_ref[6] = p_re
    tab_ref[7] = p_im


def _s5_prep(lam_re, lam_im, log_step, b_re, b_im):
    lanes = S5_LANES
    row = lambda a: a.astype(F32).reshape(1, lanes)
    ls = jnp.repeat(log_step.astype(F32), S5_STATE).reshape(1, lanes)
    bt = lambda a: a.astype(F32).reshape(lanes, S5_GROUP).T
    full = lambda shape: pl.BlockSpec(shape, lambda: tuple(0 for _ in shape))
    return pl.pallas_call(
        _s5_prep_kernel,
        in_specs=[full((1, lanes))] * 3 + [full((S5_GROUP, lanes))] * 2,
        out_specs=[full((S5_GROUP, lanes))] * 2 + [full((8, 8, lanes))],
        out_shape=[jax.ShapeDtypeStruct((S5_GROUP, lanes), F32)] * 2
        + [jax.ShapeDtypeStruct((8, 8, lanes), F32)],
        name="s5_prep",
    )(row(lam_re), row(lam_im), ls, bt(b_re), bt(b_im))


def _s5_kernel(u_ref, bbd_ref, tab_ref, cre_ref, cim_ref, d_ref, wg_ref, bg_ref, o_ref,
               xr_scr, xi_scr, car_scr):
    tt = u_ref.shape[1]
    lanes = S5_LANES
    lw = 512

    @pl.when(pl.program_id(1) == 0)
    def _():
        car_scr[...] = jnp.zeros_like(car_scr)

    u = u_ref[0]
    bu = _dot(u.astype(BF16), bbd_ref[...])
    xr_scr[...] = bu[:, :lanes]
    xi_scr[...] = bu[:, lanes:]

    for lc in range(lanes // lw):
        sl = slice(lw * lc, lw * (lc + 1))
        steps = [(d, tab_ref[2 * s][:, sl], tab_ref[2 * s + 1][:, sl]) for s, d in enumerate((1, 2, 4))]
        p_re = tab_ref[6][:, sl]
        p_im = tab_ref[7][:, sl]

        def body(k, carry, sl=sl, steps=steps, p_re=p_re, p_im=p_im):
            c_re, c_im = carry
            r0 = pl.multiple_of(k * 8, 8)
            xr = xr_scr[pl.ds(r0, 8), sl]
            xi = xi_scr[pl.ds(r0, 8), sl]
            for d, a_re, a_im in steps:
                sr = pltpu.roll(xr, d, 0)
                si = pltpu.roll(xi, d, 0)
                xr, xi = xr + a_re * sr - a_im * si, xi + a_re * si + a_im * sr
            xr, xi = xr + p_re * c_re - p_im * c_im, xi + p_re * c_im + p_im * c_re
            xr_scr[pl.ds(r0, 8), sl] = xr
            xi_scr[pl.ds(r0, 8), sl] = xi
            return xr[7:8, :], xi[7:8, :]

        c_re, c_im = lax.fori_loop(0, tt // 8, body, (car_scr[0:1, sl], car_scr[1:2, sl]))
        car_scr[0:1, sl] = c_re
        car_scr[1:2, sl] = c_im

    y = (_dot(xr_scr[...].astype(BF16), cre_ref[...]) - _dot(xi_scr[...].astype(BF16), cim_ref[...])
         + d_ref[...] * u)
    y = _gelu(y)
    o_ref[0] = (y * _sigmoid(_dot(y.astype(BF16), wg_ref[...]) + bg_ref[...])).astype(o_ref.dtype)


def _s5(proj3, bbd, tab, cre, cim, dskip, wg, bg, tt=256):
    bsz, length, _ = proj3.shape
    lanes = S5_LANES
    const = lambda shape: pl.BlockSpec(shape, lambda bi, ti: tuple(0 for _ in shape))
    return pl.pallas_call(
        _s5_kernel,
        grid=(bsz, length // tt),
        in_specs=[pl.BlockSpec((1, tt, S5_WIDTH), lambda bi, ti: (bi, ti, 4)),
                  const((S5_WIDTH, 2 * lanes)), const((8, 8, lanes)),
                  const((lanes, S5_WIDTH)), const((lanes, S5_WIDTH)),
                  const((1, S5_WIDTH)), const((S5_WIDTH, S5_WIDTH)), const((1, S5_WIDTH))],
        out_specs=pl.BlockSpec((1, tt, S5_WIDTH), lambda bi, ti: (bi, ti, 0)),
        out_shape=jax.ShapeDtypeStruct((bsz, length, S5_WIDTH), BF16),
        scratch_shapes=[pltpu.VMEM((tt, lanes), F32), pltpu.VMEM((tt, lanes), F32),
                        pltpu.VMEM((8, lanes), F32)],
        compiler_params=_params(("parallel", "arbitrary")),
        name="s5",
    )(proj3, bbd, tab, cre, cim, dskip, wg, bg)


def _block_diag_in(bb_t):
    g, c, p = S5_GROUPS, S5_GROUP, S5_STATE
    eye = jnp.eye(g, dtype=F32)
    full = eye[:, None, :, None] * bb_t.reshape(c, g, p).transpose(1, 0, 2)[:, :, None, :]
    return full.reshape(g * c, g * p)


def _block_diag_out(cmat):
    g, c, p = S5_GROUPS, S5_GROUP, S5_STATE
    eye = jnp.eye(g, dtype=F32)
    full = eye[:, None, :, None] * cmat.astype(F32).transpose(0, 2, 1)[:, :, None, :]
    return full.reshape(g * p, g * c)


def _rope_kernel(pos_ref, cos_ref, sin_ref):
    pos = pos_ref[0].astype(F32)
    half = HEAD_DIM // 2
    lane = lax.broadcasted_iota(jnp.int32, (1, HEAD_DIM), 1)
    j = (lane & (half - 1)).astype(F32)
    inv = jnp.exp(j * (-math.log(ROPE_BASE) / half))
    ang = pos * inv
    cos_ref[0] = jnp.cos(ang)
    sin_ref[0] = jnp.where(lane < half, -1.0, 1.0) * jnp.sin(ang)


def _rope_tables(positions, tr=1024):
    bsz, length = positions.shape
    tr = min(tr, length)
    spec_o = pl.BlockSpec((1, tr, HEAD_DIM), lambda bi, ti: (bi, ti, 0))
    return pl.pallas_call(
        _rope_kernel,
        grid=(bsz, length // tr),
        in_specs=[pl.BlockSpec((1, tr, 1), lambda bi, ti: (bi, ti, 0))],
        out_specs=[spec_o, spec_o],
        out_shape=[jax.ShapeDtypeStruct((bsz, length, HEAD_DIM), F32)] * 2,
        compiler_params=_params(("parallel", "parallel")),
        name="rope_tables",
    )(positions.reshape(bsz, length, 1))


def _ret_kernel(q_ref, k_ref, v_ref, g_ref, cos_ref, sin_ref, o_ref, s_scr):
    c = RET_CHUNK

    @pl.when(pl.program_id(1) == 0)
    def _():
        s_scr[...] = jnp.zeros_like(s_scr)

    cosf = cos_ref[0]
    sinf = sin_ref[0]
    rel = (lax.broadcasted_iota(jnp.int32, (c, c), 0) - lax.broadcasted_iota(jnp.int32, (c, c), 1)).astype(F32)
    idx = lax.broadcasted_iota(jnp.int32, (c, 1), 0).astype(F32)
    qa, ka, va, ga = q_ref[0], k_ref[0], v_ref[0], g_ref[0]
    outs = []
    for hh in range(HEADS):
        sl = slice(HEAD_DIM * hh, HEAD_DIM * (hh + 1))
        lg = math.log1p(-2.0 ** (-5.0 - hh))
        q = qa[:, sl]
        k = ka[:, sl]
        vb = va[:, sl].astype(BF16)
        qh = q * cosf + pltpu.roll(q, HEAD_DIM // 2, 1) * sinf
        kh = (k * cosf + pltpu.roll(k, HEAD_DIM // 2, 1) * sinf) * HEAD_DIM ** -0.5
        dmat = jnp.where(rel >= 0.0, jnp.exp(lg * jnp.maximum(rel, 0.0)), 0.0)
        scores = _dot_nt(qh.astype(BF16), kh.astype(BF16)) * dmat
        o = _dot(scores.astype(BF16), vb)
        q_dec = qh * jnp.exp(lg * (idx + 1.0))
        k_dec = kh * jnp.exp(lg * (c - 1.0 - idx))
        st = s_scr[hh]
        o = o + _dot(q_dec.astype(BF16), st.astype(BF16))
        s_scr[hh] = math.exp(lg * c) * st + _dot_tn(k_dec.astype(BF16), vb)
        mu = jnp.mean(o, axis=-1, keepdims=True)
        var = jnp.mean((o - mu) ** 2, axis=-1, keepdims=True)
        outs.append((o - mu) * lax.rsqrt(var + EPS))
    o_ref[0] = (jnp.concatenate(outs, axis=1) * _silu(ga)).astype(o_ref.dtype)


def _retention(proj3, cos_t, sin_t):
    bsz, length, _ = proj3.shape
    c = RET_CHUNK
    col = lambda k: pl.BlockSpec((1, c, MIX), lambda bi, ti, k=k: (bi, ti, k))
    tab = pl.BlockSpec((1, c, HEAD_DIM), lambda bi, ti: (bi, ti, 0))
    return pl.pallas_call(
        _ret_kernel,
        grid=(bsz, length // c),
        in_specs=[col(5), col(6), col(7), col(8), tab, tab],
        out_specs=pl.BlockSpec((1, c, MIX), lambda bi, ti: (bi, ti, 0)),
        out_shape=jax.ShapeDtypeStruct((bsz, length, MIX), BF16),
        scratch_shapes=[pltpu.VMEM((HEADS, HEAD_DIM, HEAD_DIM), F32)],
        compiler_params=_params(("parallel", "arbitrary")),
        name="retention",
    )(proj3, proj3, proj3, proj3, cos_t, sin_t)


def _memkv_kernel(m_ref, g_ref, w_ref, k_ref, v_ref):
    kv = _dot(_rms(m_ref[0], g_ref[...]).astype(BF16), w_ref[...])
    k_ref[0] = kv[:, :D_MODEL].astype(BF16)
    v_ref[0] = kv[:, D_MODEL:].astype(BF16)


def _memkv(mem, gain, wkv_bf16):
    bsz, m, d = mem.shape
    spec_o = pl.BlockSpec((1, m, d), lambda bi: (bi, 0, 0))
    return pl.pallas_call(
        _memkv_kernel,
        grid=(bsz,),
        in_specs=[pl.BlockSpec((1, m, d), lambda bi: (bi, 0, 0)),
                  pl.BlockSpec((1, d), lambda bi: (0, 0)),
                  pl.BlockSpec((d, 2 * d), lambda bi: (0, 0))],
        out_specs=[spec_o, spec_o],
        out_shape=[jax.ShapeDtypeStruct((bsz, m, d), BF16)] * 2,
        compiler_params=_params(("parallel",)),
        name="mem_kv",
    )(mem, gain, wkv_bf16)


def _mix_kernel(x_ref, ga_ref, gb_ref, bg_ref, hg_ref, s5_ref, rt_ref, wb_ref, wout_ref,
                nc_ref, k_ref, v_ref, wq_ref, wo_ref, o_ref):
    d = D_MODEL
    logits = jnp.concatenate([ga_ref[...], gb_ref[...]], axis=1) + bg_ref[...]
    gates = _sigmoid(logits)
    merged = (gates[:, 0:d] * _dot(hg_ref[...], wb_ref[0:MIX, :])
              + gates[:, d:2 * d] * _dot(s5_ref[...], wb_ref[MIX:MIX + S5_WIDTH, :])
              + gates[:, 2 * d:3 * d] * _dot(rt_ref[...], wb_ref[MIX + S5_WIDTH:, :]))
    x1 = x_ref[...] + _dot(merged.astype(BF16), wout_ref[...])

    hc = _rms(x1, nc_ref[...]).astype(BF16)
    q = _dot(hc, wq_ref[...])
    kk = k_ref[0]
    vv = v_ref[0]
    outs = []
    for hh in range(XA_HEADS):
        sl = slice(XA_DH * hh, XA_DH * (hh + 1))
        s = _dot_nt(q[:, sl].astype(BF16), kk[:, sl]) * XA_DH ** -0.5
        p = jnp.exp(s - jnp.max(s, axis=-1, keepdims=True))
        p = p / jnp.sum(p, axis=-1, keepdims=True)
        outs.append(_dot(p.astype(BF16), vv[:, sl]))
    attn = jnp.concatenate(outs, axis=1)
    o_ref[...] = x1 + _dot(attn.astype(BF16), wo_ref[...])


def _mix(x2d, proj, bgate, o_hg, o_s5, o_rt, wb, wout, ncross, kmem, vmem, wq, wo, length, tm=512):
    m, d = x2d.shape
    per_batch = length // tm
    tok = lambda w: pl.BlockSpec((tm, w), lambda i: (i, 0))
    const = lambda shape: pl.BlockSpec(shape, lambda i: tuple(0 for _ in shape))
    gw = 1536
    mem_spec = pl.BlockSpec((1,) + kmem.shape[1:], lambda i: (i // per_batch, 0, 0))
    return pl.pallas_call(
        _mix_kernel,
        grid=(m // tm,),
        in_specs=[tok(d),
                  pl.BlockSpec((tm, gw), lambda i: (i, 3)), pl.BlockSpec((tm, gw), lambda i: (i, 4)),
                  const((1, 3 * d)), tok(MIX), tok(S5_WIDTH), tok(MIX),
                  const(wb.shape), const((d, d)), const((1, d)), mem_spec, mem_spec,
                  const((d, d)), const((d, d))],
        out_specs=tok(d),
        out_shape=jax.ShapeDtypeStruct((m, d), F32),
        compiler_params=_params(("parallel",)),
        name="merge_xattn",
    )(x2d, proj, proj, bgate, o_hg, o_s5, o_rt, wb, wout, ncross, kmem, vmem, wq, wo)


def _top_extract(s):
    t = s.shape[1]
    row16 = lax.broadcasted_iota(jnp.int32, (PEER_TOPK, t), 0)
    vals = jnp.zeros((PEER_TOPK, t), F32)
    rank = jnp.full(s.shape, UNRANKED, F32)
    w = s
    for r in range(PEER_TOPK):
        m = jnp.max(w, axis=0, keepdims=True)
        hit = w == m
        rank = jnp.where(hit, float(r + 1), rank)
        w = jnp.where(hit, NEG_INF, w)
        vals = jnp.where(row16 == r, m, vals)
    return vals, rank


def _route_kernel(x_ref, g_ref, wqt_ref, keys_ref, h_ref, rank2_ref, e2_ref, n1_ref, e1_ref):
    t = x_ref.shape[0]
    k = PEER_TOPK
    hb = _rms(x_ref[...], g_ref[...]).astype(BF16)
    h_ref[...] = hb
    qt = _dot_nt(wqt_ref[...], hb).astype(BF16)
    row16 = lax.broadcasted_iota(jnp.int32, (k, t), 0)
    row8 = lax.broadcasted_iota(jnp.int32, (8, t), 0)
    for hd in range(PEER_HEADS):
        base = hd * 2 * PEER_DH
        s1 = _dot(keys_ref[2 * hd], qt[base:base + PEER_DH])
        s2 = _dot(keys_ref[2 * hd + 1], qt[base + PEER_DH:base + 2 * PEER_DH])
        a, rank1 = _top_extract(s1)
        b, rank2 = _top_extract(s2)
        cands = []
        cands.append((a + b[0:1], None))
        for j in range(2, 9):
            lim = k // j
            cands.append((jnp.where(row8 < lim, a[0:8] + b[j - 1:j], NEG_INF), None))
        tail = a[0:1] + b[8:16]
        top = a[0:1] + b[0:1]
        work = [cnd for cnd, _ in cands] + [tail]
        thr = None
        for r in range(k):
            m = work[0].max(axis=0, keepdims=True)
            for wk in work[1:]:
                m = jnp.maximum(m, wk.max(axis=0, keepdims=True))
            work = [jnp.where(wk == m, NEG_INF, wk) for wk in work]
            thr = m
        n_of_i = (cands[0][0] >= thr).astype(F32)
        zsum = jnp.sum(jnp.where(cands[0][0] >= thr, jnp.exp(cands[0][0] - top), 0.0), axis=0, keepdims=True)
        low = jnp.zeros((8, t), F32)
        for cnd, _ in cands[1:]:
            sel = cnd >= thr
            low = low + sel.astype(F32)
            zsum = zsum + jnp.sum(jnp.where(sel, jnp.exp(cnd - top), 0.0), axis=0, keepdims=True)
        sel = tail >= thr
        cnt_tail = jnp.sum(sel.astype(F32), axis=0, keepdims=True)
        zsum = zsum + jnp.sum(jnp.where(sel, jnp.exp(tail - top), 0.0), axis=0, keepdims=True)
        n_of_i = n_of_i + jnp.concatenate([low, jnp.zeros((8, t), F32)], axis=0)
        n_of_i = n_of_i + jnp.where(row16 == 0, cnt_tail, 0.0)
        n1 = jnp.zeros(s1.shape, F32)
        for i in range(k):
            n1 = jnp.where(rank1 == float(i + 1), n_of_i[i:i + 1], n1)
        rows = slice(hd * PEER_NKEYS, (hd + 1) * PEER_NKEYS)
        rank2_ref[rows, :] = rank2.astype(BF16)
        e2_ref[rows, :] = (0.5 * jnp.exp(s2 - b[0:1]) / zsum).astype(BF16)
        n1_ref[rows, :] = n1
        e1_ref[rows, :] = jnp.exp(s1 - a[0:1])


def _route(x2d, gain, wqt, keys2, tt=256):
    m, d = x2d.shape
    rows = PEER_HEADS * PEER_NKEYS
    tab = pl.BlockSpec((rows, tt), lambda i: (0, i))
    return pl.pallas_call(
        _route_kernel,
        grid=(m // tt,),
        in_specs=[pl.BlockSpec((tt, d), lambda i: (i, 0)),
                  pl.BlockSpec((1, d), lambda i: (0, 0)),
                  pl.BlockSpec(wqt.shape, lambda i: (0, 0)),
                  pl.BlockSpec(keys2.shape, lambda i: (0, 0, 0))],
        out_specs=[pl.BlockSpec((tt, d), lambda i: (i, 0)), tab, tab, tab, tab],
        out_shape=[jax.ShapeDtypeStruct((m, d), BF16)] + [jax.ShapeDtypeStruct((rows, m), BF16)] * 2
        + [jax.ShapeDtypeStruct((rows, m), F32)] * 2,
        compiler_params=_params(("parallel",)),
        name="peer_route",
    )(x2d, gain, wqt, keys2)


def _peer_kernel(final, x_ref, h_ref, rank2_ref, e2_ref, n1_ref, e1_ref, u_ref, vt_ref, gf_ref,
                 o_ref, acc_scr):
    j = pl.program_id(1)
    eb = u_ref.shape[0]
    nk = PEER_NKEYS
    chunk = 512
    per_chunk = chunk // nk

    @pl.when(j == 0)
    def _():
        acc_scr[...] = jnp.zeros_like(acc_scr)

    tt = h_ref.shape[0]
    half = tt // 2
    i1_base = pl.multiple_of(j * (eb // nk), eb // nk)
    n_chunks = eb // chunk
    assert per_chunk == 4

    def score(ck, th):
        return _dot_nt(u_ref[ck * chunk:(ck + 1) * chunk, :], h_ref[th * half:(th + 1) * half, :])

    def key_row(ref, hd, il):
        r = ref[pl.ds(hd * nk + i1_base + il, 1), :]
        return jnp.tile(jnp.broadcast_to(r, (BF16_ROWS, tt)).astype(BF16), (nk // BF16_ROWS, 1))

    def gated(z_halves, ck, sub):
        il = ck * per_chunk + sub
        gate = None
        for hd in range(PEER_HEADS):
            rows = slice(hd * nk, (hd + 1) * nk)
            sel = jnp.where(rank2_ref[rows, :] <= key_row(n1_ref, hd, il), e2_ref[rows, :], 0.0)
            term = sel * key_row(e1_ref, hd, il)
            gate = term if gate is None else gate + term
        out = []
        for th in range(2):
            zz = z_halves[th][sub * nk:(sub + 1) * nk, :]
            out.append(_gelu2(zz).astype(BF16) * gate[:, th * half:(th + 1) * half])
        return out

    def apply(ck, th, act):
        cols = slice(th * half, (th + 1) * half)
        acc_scr[:, cols] += _dot(vt_ref[:, ck * chunk:(ck + 1) * chunk], act)

    z = [score(0, 0), score(0, 1)]
    prev = None
    for ck in range(n_chunks):
        acts = ([], [])
        z_next = [None, None]
        for sub in range(per_chunk):
            a0, a1 = gated(z, ck, sub)
            acts[0].append(a0)
            acts[1].append(a1)
            if sub < 2 and ck + 1 < n_chunks:
                z_next[sub] = score(ck + 1, sub)
            if sub >= 2 and prev is not None:
                apply(ck - 1, sub - 2, prev[sub - 2])
        prev = [jnp.concatenate(acts[0], axis=0), jnp.concatenate(acts[1], axis=0)]
        z = z_next
    apply(n_chunks - 1, 0, prev[0])
    apply(n_chunks - 1, 1, prev[1])

    @pl.when(j == pl.num_programs(1) - 1)
    def _():
        y = x_ref[...] + acc_scr[...].T
        if final:
            y = _rms(y, gf_ref[...])
        o_ref[...] = y


def _peer(x2d, h, rank2, e2, n1, e1, u_bf16, vt_bf16, gfinal, final, tt=512, eb=2048):
    m, d = x2d.shape
    ne = u_bf16.shape[0]
    rows = PEER_HEADS * PEER_NKEYS
    tab = pl.BlockSpec((rows, tt), lambda i, j: (0, i))
    return pl.pallas_call(
        functools.partial(_peer_kernel, final),
        grid=(m // tt, ne // eb),
        in_specs=[pl.BlockSpec((tt, d), lambda i, j: (i, 0)),
                  pl.BlockSpec((tt, d), lambda i, j: (i, 0)),
                  tab, tab, tab, tab,
                  pl.BlockSpec((eb, d), lambda i, j: (j, 0)),
                  pl.BlockSpec((d, eb), lambda i, j: (0, j)),
                  pl.BlockSpec((1, d), lambda i, j: (0, 0))],
        out_specs=pl.BlockSpec((tt, d), lambda i, j: (i, 0)),
        out_shape=jax.ShapeDtypeStruct((m, d), F32),
        scratch_shapes=[pltpu.VMEM((d, tt), F32)],
        compiler_params=_params(("parallel", "arbitrary")),
        name="peer_experts",
    )(x2d, h, rank2, e2, n1, e1, u_bf16, vt_bf16, gfinal)


def kernel(x, mem, positions, norm_mix, w_in, b_gate, hgrn_lb, hgrn_norm, s5_lambda_re, s5_lambda_im,
           s5_log_step, s5_b_re, s5_b_im, s5_c_re, s5_c_im, s5_d, s5_w_glu, s5_b_glu, w_branch, w_out,
           norm_cross, norm_mem, xa_wq, xa_wkv, xa_wo, norm_ffn, peer_wq, peer_keys, peer_u, peer_v,
           norm_final):
    bsz, length, d = x.shape
    depth = w_in.shape[0]
    row = lambda a: a.astype(F32).reshape(1, -1)
    x2d = x.astype(F32).reshape(bsz * length, d)
    cos_t, sin_t = _rope_tables(positions)
    lb = hgrn_lb.astype(F32)
    for l in range(depth):
        proj = _inproj(x2d, row(norm_mix[l]), w_in[l].astype(BF16))
        proj3 = proj.reshape(bsz, length, IN_COLS)
        o_hg = _hgrn(proj3, lb, row(hgrn_norm[l]), l)
        bb_re, bb_im, tab = _s5_prep(s5_lambda_re[l], s5_lambda_im[l], s5_log_step[l], s5_b_re[l], s5_b_im[l])
        bbd = jnp.concatenate([_block_diag_in(bb_re), _block_diag_in(bb_im)], axis=1).astype(BF16)
        o_s5 = _s5(proj3, bbd, tab, _block_diag_out(s5_c_re[l]).astype(BF16),
                   _block_diag_out(s5_c_im[l]).astype(BF16), row(s5_d[l]),
                   s5_w_glu[l].astype(BF16), row(s5_b_glu[l]))
        o_rt = _retention(proj3, cos_t, sin_t)
        kmem, vmem = _memkv(mem.astype(F32), row(norm_mem[l]), xa_wkv[l].astype(BF16))
        x2d = _mix(x2d, proj, row(b_gate[l]), o_hg.reshape(-1, MIX), o_s5.reshape(-1, S5_WIDTH),
                   o_rt.reshape(-1, MIX), w_branch[l].astype(BF16), w_out[l].astype(BF16),
                   row(norm_cross[l]), kmem, vmem, xa_wq[l].astype(BF16), xa_wo[l].astype(BF16), length)
        keys2 = peer_keys[l].astype(BF16).reshape(PEER_HEADS * 2, PEER_NKEYS, PEER_DH)
        h, rank2, e2, n1, e1 = _route(x2d, row(norm_ffn[l]), peer_wq[l].astype(BF16).T, keys2)
        x2d = _peer(x2d, h, rank2, e2, n1, e1, peer_u[l].astype(BF16), peer_v[l].astype(BF16).T,
                    row(norm_final), final=(l == depth - 1))
    return x2d.reshape(bsz, length, d)
```

```python
import functools
import math

import jax
import jax.numpy as jnp
from jax import lax
from jax.experimental import pallas as pl
from jax.experimental.pallas import tpu as pltpu

F32 = jnp.float32
BF16 = jnp.bfloat16

D_MODEL = 1024
HEADS = 4
HEAD_DIM = 128
MIX = HEADS * HEAD_DIM
HG_CHUNK = 16
HG_F_MIN = 1e-6
S5_GROUPS = 32
S5_GROUP = 16
S5_STATE = 64
S5_WIDTH = S5_GROUPS * S5_GROUP
S5_LANES = S5_GROUPS * S5_STATE
RET_CHUNK = 128
ROPE_BASE = 10000.0
IN_COLS = 4 * MIX + S5_WIDTH + 4 * MIX + 3 * D_MODEL
XA_HEADS = 4
XA_DH = D_MODEL // XA_HEADS
PEER_HEADS = 8
PEER_NKEYS = 128
PEER_DH = 128
PEER_TOPK = 16
EPS = 1e-6
NEG_INF = float("-inf")
UNRANKED = 99.0
BF16_ROWS = 16
LANES = 128

VMEM_LIMIT = 56 * 1024 * 1024

NT_DIMS = (((1,), (1,)), ((), ()))
TN_DIMS = (((0,), (0,)), ((), ()))


def _params(sem):
    return pltpu.CompilerParams(dimension_semantics=sem, vmem_limit_bytes=VMEM_LIMIT)


def _rms(x, gain):
    return x * lax.rsqrt(jnp.mean(x * x, axis=-1, keepdims=True) + EPS) * gain


def _gelu(x):
    return 0.5 * x * (1.0 + jnp.tanh(math.sqrt(2.0 / math.pi) * (x + 0.044715 * (x * x * x))))


def _gelu2(x):
    c = math.sqrt(2.0 / math.pi)
    return x + x * jnp.tanh(x * (c + (c * 0.044715) * (x * x)))


def _sigmoid(x):
    return 1.0 / (1.0 + jnp.exp(-x))


def _silu(x):
    return x * _sigmoid(x)


def _dot(a, b):
    return jnp.dot(a, b, preferred_element_type=F32)


def _dot_nt(a, b):
    return lax.dot_general(a, b, NT_DIMS, preferred_element_type=F32)


def _dot_tn(a, b):
    return lax.dot_general(a, b, TN_DIMS, preferred_element_type=F32)


def _inproj_kernel(x_ref, g_ref, w_ref, o_ref, h_scr):
    @pl.when(pl.program_id(1) == 0)
    def _():
        h_scr[...] = _rms(x_ref[...], g_ref[...]).astype(BF16)

    o_ref[...] = _dot(h_scr[...], w_ref[...])


def _inproj(x2d, gain, w_bf16, tm=1024, tn=1536):
    m, d = x2d.shape
    n = w_bf16.shape[1]
    return pl.pallas_call(
        _inproj_kernel,
        grid=(m // tm, n // tn),
        in_specs=[
            pl.BlockSpec((tm, d), lambda i, j: (i, 0)),
            pl.BlockSpec((1, d), lambda i, j: (0, 0)),
            pl.BlockSpec((d, tn), lambda i, j: (0, j)),
        ],
        out_specs=pl.BlockSpec((tm, tn), lambda i, j: (i, j)),
        out_shape=jax.ShapeDtypeStruct((m, n), F32),
        scratch_shapes=[pltpu.VMEM((tm, d), BF16)],
        compiler_params=_params(("parallel", "arbitrary")),
        name="inproj",
    )(x2d, gain, w_bf16)


def _hgrn_kernel(layer, q_ref, f_ref, i_ref, g_ref, lb_ref, ng_ref, ones_ref, o_ref,
                 st_scr, b_scr, q_scr, k_scr, v_scr, oi_scr):
    tc = q_ref.shape[1]
    c = HG_CHUNK

    @pl.when(pl.program_id(1) == 0)
    def _():
        st_scr[...] = jnp.zeros_like(st_scr)

    lbs = lb_ref[...]
    e = jnp.exp(lbs - jnp.max(lbs, axis=0, keepdims=True))
    sm = e / jnp.sum(e, axis=0, keepdims=True)
    lower = jnp.sum(sm[0:layer + 1], axis=0, keepdims=True) - sm[0:1]

    z = f_ref[0]
    fv = lower + (1.0 - lower) * _sigmoid(z)
    logf = jnp.log(jnp.clip(fv, HG_F_MIN, 1.0))
    kk = (1.0 - lower) * _sigmoid(-z)
    qs = q_ref[0] * HEAD_DIM ** -0.5
    v = i_ref[0]

    rin = lax.broadcasted_iota(jnp.int32, (tc, MIX), 0) & (c - 1)
    b = logf
    d = 1
    while d < c:
        b = b + jnp.where(rin >= d, pltpu.roll(b, d, 0), 0.0)
        d *= 2

    ones2 = ones_ref[...]
    o = jnp.zeros((tc, MIX), F32)
    for off in range(c):
        if off == 0:
            bs, ks, vs = b, kk, v
        else:
            bs, ks, vs = pltpu.roll(b, off, 0), pltpu.roll(kk, off, 0), pltpu.roll(v, off, 0)
        dec = jnp.exp(jnp.minimum(b - bs, 0.0))
        prod = jnp.where(rin >= off, dec * qs * ks, 0.0).astype(BF16)
        att = jnp.concatenate(
            [_dot(prod[:, 0:256], ones2), _dot(prod[:, 256:512], ones2)], axis=1)
        o = o + att * vs

    b_scr[...] = b
    q_scr[...] = qs
    k_scr[...] = kk
    v_scr[...] = v

    def chunk(ci, carry):
        r0 = pl.multiple_of(ci * c, c)
        bc = b_scr[pl.ds(r0, c), :]
        bl = bc[c - 1:c, :]
        qd = (q_scr[pl.ds(r0, c), :] * jnp.exp(bc)).astype(BF16)
        kd = (k_scr[pl.ds(r0, c), :] * jnp.exp(bl - bc)).astype(BF16)
        vc = v_scr[pl.ds(r0, c), :].astype(BF16)
        dec_l = jnp.exp(bl)
        for hh in range(HEADS):
            sl = slice(HEAD_DIM * hh, HEAD_DIM * (hh + 1))
            st = st_scr[hh]
            oi_scr[pl.ds(r0, c), sl] = _dot_nt(qd[:, sl], st.astype(BF16))
            st_scr[hh] = st * dec_l[:, sl] + _dot_tn(vc[:, sl], kd[:, sl])
        return carry

    lax.fori_loop(0, tc // c, chunk, 0, unroll=8)
    o = o + oi_scr[...]

    gate = g_ref[0]
    ng = ng_ref[...]
    outs = []
    for hh in range(HEADS):
        sl = slice(HEAD_DIM * hh, HEAD_DIM * (hh + 1))
        oh = o[:, sl]
        outs.append(oh * lax.rsqrt(jnp.mean(oh * oh, axis=-1, keepdims=True) + EPS) * ng[:, sl])
    o_ref[0] = (jnp.concatenate(outs, axis=1) * _silu(gate)).astype(o_ref.dtype)


def _hgrn(proj3, lb, ng, layer, tc=256):
    bsz, length, _ = proj3.shape
    ones2 = jnp.kron(jnp.eye(2, dtype=F32), jnp.ones((HEAD_DIM, HEAD_DIM), F32)).astype(BF16)
    col = lambda k: pl.BlockSpec((1, tc, MIX), lambda bi, ti, k=k: (bi, ti, k))
    return pl.pallas_call(
        functools.partial(_hgrn_kernel, layer),
        grid=(bsz, length // tc),
        in_specs=[col(0), col(1), col(2), col(3),
                  pl.BlockSpec(lb.shape, lambda bi, ti: (0, 0)),
                  pl.BlockSpec((1, MIX), lambda bi, ti: (0, 0)),
                  pl.BlockSpec((256, 256), lambda bi, ti: (0, 0))],
        out_specs=pl.BlockSpec((1, tc, MIX), lambda bi, ti: (bi, ti, 0)),
        out_shape=jax.ShapeDtypeStruct((bsz, length, MIX), BF16),
        scratch_shapes=[pltpu.VMEM((HEADS, HEAD_DIM, HEAD_DIM), F32)]
        + [pltpu.VMEM((tc, MIX), F32)] * 5,
        compiler_params=_params(("parallel", "arbitrary")),
        name="hgrn2",
    )(proj3, proj3, proj3, proj3, lb, ng, ones2)


def _s5_prep_kernel(lr_ref, li_ref, ls_ref, bre_ref, bim_ref, bbre_ref, bbim_ref, tab_ref):
    lr = jnp.minimum(lr_ref[...], -1e-4)
    li = li_ref[...]
    dt = jnp.exp(ls_ref[...])
    mag = jnp.exp(lr * dt)
    a_re = mag * jnp.cos(li * dt)
    a_im = mag * jnp.sin(li * dt)
    den = lr * lr + li * li
    z_re = ((a_re - 1.0) * lr + a_im * li) / den
    z_im = (a_im * lr - (a_re - 1.0) * li) / den
    bre = bre_ref[...]
    bim = bim_ref[...]
    bbre_ref[...] = z_re * bre - z_im * bim
    bbim_ref[...] = z_re * bim + z_im * bre
    row = lax.broadcasted_iota(jnp.int32, (8, a_re.shape[1]), 0)
    p_re = jnp.broadcast_to(a_re, row.shape)
    p_im = jnp.broadcast_to(a_im, row.shape)
    c_re, c_im = a_re, a_im
    powers = [(a_re, a_im)]
    for r in range(1, 8):
        c_re, c_im = c_re * a_re - c_im * a_im, c_re * a_im + c_im * a_re
        powers.append((c_re, c_im))
        p_re = jnp.where(row == r, c_re, p_re)
        p_im = jnp.where(row == r, c_im, p_im)
    for slot, d in enumerate((1, 2, 4)):
        tab_ref[2 * slot] = jnp.where(row >= d, powers[d - 1][0], 0.0)
        tab_ref[2 * slot + 1] = jnp.where(row >= d, powers[d - 1][1], 0.0)
    tab_ref[6] = p_re
    tab_ref[7] = p_im


def _s5_prep(lam_re, lam_im, log_step, b_re, b_im):
    lanes = S5_LANES
    row = lambda a: a.astype(F32).reshape(1, lanes)
    ls = jnp.repeat(log_step.astype(F32), S5_STATE).reshape(1, lanes)
    bt = lambda a: a.astype(F32).reshape(lanes, S5_GROUP).T
    full = lambda shape: pl.BlockSpec(shape, lambda: tuple(0 for _ in shape))
    return pl.pallas_call(
        _s5_prep_kernel,
        in_specs=[full((1, lanes))] * 3 + [full((S5_GROUP, lanes))] * 2,
        out_specs=[full((S5_GROUP, lanes))] * 2 + [full((8, 8, lanes))],
        out_shape=[jax.ShapeDtypeStruct((S5_GROUP, lanes), F32)] * 2
        + [jax.ShapeDtypeStruct((8, 8, lanes), F32)],
        name="s5_prep",
    )(row(lam_re), row(lam_im), ls, bt(b_re), bt(b_im))


def _s5_kernel(u_ref, bbd_ref, tab_ref, cre_ref, cim_ref, d_ref, wg_ref, bg_ref, o_ref,
               xr_scr, xi_scr, car_scr):
    tt = u_ref.shape[1]
    lanes = S5_LANES
    lw = 512

    @pl.when(pl.program_id(1) == 0)
    def _():
        car_scr[...] = jnp.zeros_like(car_scr)

    u = u_ref[0]
    bu = _dot(u.astype(BF16), bbd_ref[...])
    xr_scr[...] = bu[:, :lanes]
    xi_scr[...] = bu[:, lanes:]

    for lc in range(lanes // lw):
        sl = slice(lw * lc, lw * (lc + 1))
        steps = [(d, tab_ref[2 * s][:, sl], tab_ref[2 * s + 1][:, sl]) for s, d in enumerate((1, 2, 4))]
        p_re = tab_ref[6][:, sl]
        p_im = tab_ref[7][:, sl]

        def body(k, carry, sl=sl, steps=steps, p_re=p_re, p_im=p_im):
            c_re, c_im = carry
            r0 = pl.multiple_of(k * 8, 8)
            xr = xr_scr[pl.ds(r0, 8), sl]
            xi = xi_scr[pl.ds(r0, 8), sl]
            for d, a_re, a_im in steps:
                sr = pltpu.roll(xr, d, 0)
                si = pltpu.roll(xi, d, 0)
                xr, xi = xr + a_re * sr - a_im * si, xi + a_re * si + a_im * sr
            xr, xi = xr + p_re * c_re - p_im * c_im, xi + p_re * c_im + p_im * c_re
            xr_scr[pl.ds(r0, 8), sl] = xr
            xi_scr[pl.ds(r0, 8), sl] = xi
            return xr[7:8, :], xi[7:8, :]

        c_re, c_im = lax.fori_loop(0, tt // 8, body, (car_scr[0:1, sl], car_scr[1:2, sl]))
        car_scr[0:1, sl] = c_re
        car_scr[1:2, sl] = c_im

    y = (_dot(xr_scr[...].astype(BF16), cre_ref[...]) - _dot(xi_scr[...].astype(BF16), cim_ref[...])
         + d_ref[...] * u)
    y = _gelu(y)
    o_ref[0] = (y * _sigmoid(_dot(y.astype(BF16), wg_ref[...]) + bg_ref[...])).astype(o_ref.dtype)


def _s5(proj3, bbd, tab, cre, cim, dskip, wg, bg, tt=256):
    bsz, length, _ = proj3.shape
    lanes = S5_LANES
    const = lambda shape: pl.BlockSpec(shape, lambda bi, ti: tuple(0 for _ in shape))
    return pl.pallas_call(
        _s5_kernel,
        grid=(bsz, length // tt),
        in_specs=[pl.BlockSpec((1, tt, S5_WIDTH), lambda bi, ti: (bi, ti, 4)),
                  const((S5_WIDTH, 2 * lanes)), const((8, 8, lanes)),
                  const((lanes, S5_WIDTH)), const((lanes, S5_WIDTH)),
                  const((1, S5_WIDTH)), const((S5_WIDTH, S5_WIDTH)), const((1, S5_WIDTH))],
        out_specs=pl.BlockSpec((1, tt, S5_WIDTH), lambda bi, ti: (bi, ti, 0)),
        out_shape=jax.ShapeDtypeStruct((bsz, length, S5_WIDTH), BF16),
        scratch_shapes=[pltpu.VMEM((tt, lanes), F32), pltpu.VMEM((tt, lanes), F32),
                        pltpu.VMEM((8, lanes), F32)],
        compiler_params=_params(("parallel", "arbitrary")),
        name="s5",
    )(proj3, bbd, tab, cre, cim, dskip, wg, bg)


def _block_diag_in(bb_t):
    g, c, p = S5_GROUPS, S5_GROUP, S5_STATE
    eye = jnp.eye(g, dtype=F32)
    full = eye[:, None, :, None] * bb_t.reshape(c, g, p).transpose(1, 0, 2)[:, :, None, :]
    return full.reshape(g * c, g * p)


def _block_diag_out(cmat):
    g, c, p = S5_GROUPS, S5_GROUP, S5_STATE
    eye = jnp.eye(g, dtype=F32)
    full = eye[:, None, :, None] * cmat.astype(F32).transpose(0, 2, 1)[:, :, None, :]
    return full.reshape(g * p, g * c)


def _rope_kernel(pos_ref, cos_ref, sin_ref):
    pos = pos_ref[0].astype(F32)
    half = HEAD_DIM // 2
    lane = lax.broadcasted_iota(jnp.int32, (1, HEAD_DIM), 1)
    j = (lane & (half - 1)).astype(F32)
    inv = jnp.exp(j * (-math.log(ROPE_BASE) / half))
    ang = pos * inv
    cos_ref[0] = jnp.cos(ang)
    sin_ref[0] = jnp.where(lane < half, -1.0, 1.0) * jnp.sin(ang)


def _rope_tables(positions, tr=1024):
    bsz, length = positions.shape
    tr = min(tr, length)
    spec_o = pl.BlockSpec((1, tr, HEAD_DIM), lambda bi, ti: (bi, ti, 0))
    return pl.pallas_call(
        _rope_kernel,
        grid=(bsz, length // tr),
        in_specs=[pl.BlockSpec((1, tr, 1), lambda bi, ti: (bi, ti, 0))],
        out_specs=[spec_o, spec_o],
        out_shape=[jax.ShapeDtypeStruct((bsz, length, HEAD_DIM), F32)] * 2,
        compiler_params=_params(("parallel", "parallel")),
        name="rope_tables",
    )(positions.reshape(bsz, length, 1))


def _ret_kernel(q_ref, k_ref, v_ref, g_ref, cos_ref, sin_ref, o_ref, s_scr):
    c = RET_CHUNK

    @pl.when(pl.program_id(1) == 0)
    def _():
        s_scr[...] = jnp.zeros_like(s_scr)

    cosf = cos_ref[0]
    sinf = sin_ref[0]
    rel = (lax.broadcasted_iota(jnp.int32, (c, c), 0) - lax.broadcasted_iota(jnp.int32, (c, c), 1)).astype(F32)
    idx = lax.broadcasted_iota(jnp.int32, (c, 1), 0).astype(F32)
    qa, ka, va, ga = q_ref[0], k_ref[0], v_ref[0], g_ref[0]
    outs = []
    for hh in range(HEADS):
        sl = slice(HEAD_DIM * hh, HEAD_DIM * (hh + 1))
        lg = math.log1p(-2.0 ** (-5.0 - hh))
        q = qa[:, sl]
        k = ka[:, sl]
        vb = va[:, sl].astype(BF16)
        qh = q * cosf + pltpu.roll(q, HEAD_DIM // 2, 1) * sinf
        kh = (k * cosf + pltpu.roll(k, HEAD_DIM // 2, 1) * sinf) * HEAD_DIM ** -0.5
        dmat = jnp.where(rel >= 0.0, jnp.exp(lg * jnp.maximum(rel, 0.0)), 0.0)
        scores = _dot_nt(qh.astype(BF16), kh.astype(BF16)) * dmat
        o = _dot(scores.astype(BF16), vb)
        q_dec = qh * jnp.exp(lg * (idx + 1.0))
        k_dec = kh * jnp.exp(lg * (c - 1.0 - idx))
        st = s_scr[hh]
        o = o + _dot(q_dec.astype(BF16), st.astype(BF16))
        s_scr[hh] = math.exp(lg * c) * st + _dot_tn(k_dec.astype(BF16), vb)
        mu = jnp.mean(o, axis=-1, keepdims=True)
        var = jnp.mean((o - mu) ** 2, axis=-1, keepdims=True)
        outs.append((o - mu) * lax.rsqrt(var + EPS))
    o_ref[0] = (jnp.concatenate(outs, axis=1) * _silu(ga)).astype(o_ref.dtype)


def _retention(proj3, cos_t, sin_t):
    bsz, length, _ = proj3.shape
    c = RET_CHUNK
    col = lambda k: pl.BlockSpec((1, c, MIX), lambda bi, ti, k=k: (bi, ti, k))
    tab = pl.BlockSpec((1, c, HEAD_DIM), lambda bi, ti: (bi, ti, 0))
    return pl.pallas_call(
        _ret_kernel,
        grid=(bsz, length // c),
        in_specs=[col(5), col(6), col(7), col(8), tab, tab],
        out_specs=pl.BlockSpec((1, c, MIX), lambda bi, ti: (bi, ti, 0)),
        out_shape=jax.ShapeDtypeStruct((bsz, length, MIX), BF16),
        scratch_shapes=[pltpu.VMEM((HEADS, HEAD_DIM, HEAD_DIM), F32)],
        compiler_params=_params(("parallel", "arbitrary")),
        name="retention",
    )(proj3, proj3, proj3, proj3, cos_t, sin_t)


def _memkv_kernel(m_ref, g_ref, w_ref, k_ref, v_ref):
    kv = _dot(_rms(m_ref[0], g_ref[...]).astype(BF16), w_ref[...])
    k_ref[0] = kv[:, :D_MODEL].astype(BF16)
    v_ref[0] = kv[:, D_MODEL:].astype(BF16)


def _memkv(mem, gain, wkv_bf16):
    bsz, m, d = mem.shape
    spec_o = pl.BlockSpec((1, m, d), lambda bi: (bi, 0, 0))
    return pl.pallas_call(
        _memkv_kernel,
        grid=(bsz,),
        in_specs=[pl.BlockSpec((1, m, d), lambda bi: (bi, 0, 0)),
                  pl.BlockSpec((1, d), lambda bi: (0, 0)),
                  pl.BlockSpec((d, 2 * d), lambda bi: (0, 0))],
        out_specs=[spec_o, spec_o],
        out_shape=[jax.ShapeDtypeStruct((bsz, m, d), BF16)] * 2,
        compiler_params=_params(("parallel",)),
        name="mem_kv",
    )(mem, gain, wkv_bf16)


def _mix_kernel(x_ref, ga_ref, gb_ref, bg_ref, hg_ref, s5_ref, rt_ref, wb_ref, wout_ref,
                nc_ref, k_ref, v_ref, wq_ref, wo_ref, o_ref):
    d = D_MODEL
    logits = jnp.concatenate([ga_ref[...], gb_ref[...]], axis=1) + bg_ref[...]
    gates = _sigmoid(logits)
    merged = (gates[:, 0:d] * _dot(hg_ref[...], wb_ref[0:MIX, :])
              + gates[:, d:2 * d] * _dot(s5_ref[...], wb_ref[MIX:MIX + S5_WIDTH, :])
              + gates[:, 2 * d:3 * d] * _dot(rt_ref[...], wb_ref[MIX + S5_WIDTH:, :]))
    x1 = x_ref[...] + _dot(merged.astype(BF16), wout_ref[...])

    hc = _rms(x1, nc_ref[...]).astype(BF16)
    q = _dot(hc, wq_ref[...])
    kk = k_ref[0]
    vv = v_ref[0]
    outs = []
    for hh in range(XA_HEADS):
        sl = slice(XA_DH * hh, XA_DH * (hh + 1))
        s = _dot_nt(q[:, sl].astype(BF16), kk[:, sl]) * XA_DH ** -0.5
        p = jnp.exp(s - jnp.max(s, axis=-1, keepdims=True))
        p = p / jnp.sum(p, axis=-1, keepdims=True)
        outs.append(_dot(p.astype(BF16), vv[:, sl]))
    attn = jnp.concatenate(outs, axis=1)
    o_ref[...] = x1 + _dot(attn.astype(BF16), wo_ref[...])


def _mix(x2d, proj, bgate, o_hg, o_s5, o_rt, wb, wout, ncross, kmem, vmem, wq, wo, length, tm=512):
    m, d = x2d.shape
    per_batch = length // tm
    tok = lambda w: pl.BlockSpec((tm, w), lambda i: (i, 0))
    const = lambda shape: pl.BlockSpec(shape, lambda i: tuple(0 for _ in shape))
    gw = 1536
    mem_spec = pl.BlockSpec((1,) + kmem.shape[1:], lambda i: (i // per_batch, 0, 0))
    return pl.pallas_call(
        _mix_kernel,
        grid=(m // tm,),
        in_specs=[tok(d),
                  pl.BlockSpec((tm, gw), lambda i: (i, 3)), pl.BlockSpec((tm, gw), lambda i: (i, 4)),
                  const((1, 3 * d)), tok(MIX), tok(S5_WIDTH), tok(MIX),
                  const(wb.shape), const((d, d)), const((1, d)), mem_spec, mem_spec,
                  const((d, d)), const((d, d))],
        out_specs=tok(d),
        out_shape=jax.ShapeDtypeStruct((m, d), F32),
        compiler_params=_params(("parallel",)),
        name="merge_xattn",
    )(x2d, proj, proj, bgate, o_hg, o_s5, o_rt, wb, wout, ncross, kmem, vmem, wq, wo)


def _top_extract(s, with_rank):
    t = s.shape[1]
    row16 = lax.broadcasted_iota(jnp.int32, (PEER_TOPK, t), 0)
    vals = jnp.zeros((PEER_TOPK, t), F32)
    rank = jnp.full(s.shape, UNRANKED, F32) if with_rank else None
    w = s
    for r in range(PEER_TOPK):
        m = jnp.max(w, axis=0, keepdims=True)
        hit = w == m
        if with_rank:
            rank = jnp.where(hit, float(r + 1), rank)
        w = jnp.where(hit, NEG_INF, w)
        vals = jnp.where(row16 == r, m, vals)
    return vals, rank


def _route_kernel(x_ref, g_ref, wqt_ref, keys_ref, h_ref, rank2_ref, e2_ref, n1_ref, e1_ref):
    t = x_ref.shape[0]
    k = PEER_TOPK
    hb = _rms(x_ref[...], g_ref[...]).astype(BF16)
    h_ref[...] = hb
    qt = _dot_nt(wqt_ref[...], hb).astype(BF16)
    row16 = lax.broadcasted_iota(jnp.int32, (k, t), 0)
    row8 = lax.broadcasted_iota(jnp.int32, (8, t), 0)
    zeros8 = jnp.zeros((8, t), F32)
    for hd in range(PEER_HEADS):
        base = hd * 2 * PEER_DH
        s1 = _dot(keys_ref[2 * hd], qt[base:base + PEER_DH])
        s2 = _dot(keys_ref[2 * hd + 1], qt[base + PEER_DH:base + 2 * PEER_DH])
        a, _ = _top_extract(s1, False)
        b, rank2 = _top_extract(s2, True)
        pieces = [a + b[0:1]]
        for j in range(2, 9):
            pieces.append(jnp.where(row8 < k // j, a[0:8] + b[j - 1:j], NEG_INF))
        pieces.append(a[0:1] + b[8:16])
        cand = jnp.concatenate(pieces, axis=0)
        top = a[0:1] + b[0:1]
        work = cand
        thr = top
        for r in range(k):
            thr = jnp.max(work, axis=0, keepdims=True)
            work = jnp.where(work == thr, NEG_INF, work)
        sel = cand >= thr
        zsum = jnp.sum(jnp.where(sel, jnp.exp(cand - top), 0.0), axis=0, keepdims=True)
        picked = sel.astype(F32)
        low = picked[16:24]
        for j in range(3, 9):
            low = low + picked[8 * j:8 * j + 8]
        cnt_tail = jnp.sum(picked[72:80], axis=0, keepdims=True)
        n_of_i = picked[0:16] + jnp.concatenate([low, zeros8], axis=0) + jnp.where(row16 == 0, cnt_tail, 0.0)
        n1 = jnp.zeros(s1.shape, F32)
        for i in range(k):
            n1 = jnp.where(s1 == a[i:i + 1], n_of_i[i:i + 1], n1)
        rows = slice(hd * PEER_NKEYS, (hd + 1) * PEER_NKEYS)
        rank2_ref[rows, :] = rank2.astype(BF16)
        e2_ref[rows, :] = (0.5 * jnp.exp(s2 - b[0:1]) / zsum).astype(BF16)
        n1_ref[rows, :] = n1
        e1_ref[rows, :] = jnp.exp(s1 - a[0:1])


def _route(x2d, gain, wqt, keys2, tt=256):
    m, d = x2d.shape
    rows = PEER_HEADS * PEER_NKEYS
    tab = pl.BlockSpec((rows, tt), lambda i: (0, i))
    return pl.pallas_call(
        _route_kernel,
        grid=(m // tt,),
        in_specs=[pl.BlockSpec((tt, d), lambda i: (i, 0)),
                  pl.BlockSpec((1, d), lambda i: (0, 0)),
                  pl.BlockSpec(wqt.shape, lambda i: (0, 0)),
                  pl.BlockSpec(keys2.shape, lambda i: (0, 0, 0))],
        out_specs=[pl.BlockSpec((tt, d), lambda i: (i, 0)), tab, tab, tab, tab],
        out_shape=[jax.ShapeDtypeStruct((m, d), BF16)] + [jax.ShapeDtypeStruct((rows, m), BF16)] * 2
        + [jax.ShapeDtypeStruct((rows, m), F32)] * 2,
        compiler_params=_params(("parallel",)),
        name="peer_route",
    )(x2d, gain, wqt, keys2)


def _peer_kernel(final, x_ref, h_ref, rank2_ref, e2_ref, n1_ref, e1_ref, u_ref, vt_ref, gf_ref,
                 o_ref, acc_scr):
    j = pl.program_id(1)
    eb = u_ref.shape[0]
    nk = PEER_NKEYS
    chunk = 512
    per_chunk = chunk // nk

    @pl.when(j == 0)
    def _():
        acc_scr[...] = jnp.zeros_like(acc_scr)

    tt = h_ref.shape[0]
    half = tt // 2
    i1_base = pl.multiple_of(j * (eb // nk), eb // nk)
    n_chunks = eb // chunk
    assert per_chunk == 4

    def score(ck, th):
        return _dot_nt(u_ref[ck * chunk:(ck + 1) * chunk, :], h_ref[th * half:(th + 1) * half, :])

    def key_row(ref, hd, il, cols):
        r = ref[pl.ds(hd * nk + i1_base + il, 1), :][:, cols]
        return jnp.tile(jnp.broadcast_to(r, (BF16_ROWS, half)).astype(BF16), (nk // BF16_ROWS, 1))

    def act_tile(z_half, ck, sub, th):
        il = ck * per_chunk + sub
        cols = slice(th * half, (th + 1) * half)
        gate = None
        for hd in range(PEER_HEADS):
            rows = slice(hd * nk, (hd + 1) * nk)
            sel = jnp.where(rank2_ref[rows, cols] <= key_row(n1_ref, hd, il, cols), e2_ref[rows, cols], 0.0)
            term = sel * key_row(e1_ref, hd, il, cols)
            gate = term if gate is None else gate + term
        return _gelu2(z_half[sub * nk:(sub + 1) * nk, :]).astype(BF16) * gate

    def apply(ck, th, act):
        cols = slice(th * half, (th + 1) * half)
        acc_scr[:, cols] += _dot(vt_ref[:, ck * chunk:(ck + 1) * chunk], act)

    z = [score(0, 0), score(0, 1)]
    prev = None
    for ck in range(n_chunks):
        acts = ([], [])
        z_next = [None, None]
        for sub in range(per_chunk):
            for th in range(2):
                acts[th].append(act_tile(z[th], ck, sub, th))
            if sub < 2 and ck + 1 < n_chunks:
                z_next[sub] = score(ck + 1, sub)
            if sub >= 2 and prev is not None:
                apply(ck - 1, sub - 2, prev[sub - 2])
        prev = [jnp.concatenate(acts[0], axis=0), jnp.concatenate(acts[1], axis=0)]
        z = z_next
    apply(n_chunks - 1, 0, prev[0])
    apply(n_chunks - 1, 1, prev[1])

    @pl.when(j == pl.num_programs(1) - 1)
    def _():
        y = x_ref[...] + acc_scr[...].T
        if final:
            y = _rms(y, gf_ref[...])
        o_ref[...] = y


def _peer(x2d, h, rank2, e2, n1, e1, u_bf16, vt_bf16, gfinal, final, tt=512, eb=2048):
    m, d = x2d.shape
    ne = u_bf16.shape[0]
    rows = PEER_HEADS * PEER_NKEYS
    tab = pl.BlockSpec((rows, tt), lambda i, j: (0, i))
    return pl.pallas_call(
        functools.partial(_peer_kernel, final),
        grid=(m // tt, ne // eb),
        in_specs=[pl.BlockSpec((tt, d), lambda i, j: (i, 0)),
                  pl.BlockSpec((tt, d), lambda i, j: (i, 0)),
                  tab, tab, tab, tab,
                  pl.BlockSpec((eb, d), lambda i, j: (j, 0)),
                  pl.BlockSpec((d, eb), lambda i, j: (0, j)),
                  pl.BlockSpec((1, d), lambda i, j: (0, 0))],
        out_specs=pl.BlockSpec((tt, d), lambda i, j: (i, 0)),
        out_shape=jax.ShapeDtypeStruct((m, d), F32),
        scratch_shapes=[pltpu.VMEM((d, tt), F32)],
        compiler_params=_params(("parallel", "arbitrary")),
        name="peer_experts",
    )(x2d, h, rank2, e2, n1, e1, u_bf16, vt_bf16, gfinal)


def kernel(x, mem, positions, norm_mix, w_in, b_gate, hgrn_lb, hgrn_norm, s5_lambda_re, s5_lambda_im,
           s5_log_step, s5_b_re, s5_b_im, s5_c_re, s5_c_im, s5_d, s5_w_glu, s5_b_glu, w_branch, w_out,
           norm_cross, norm_mem, xa_wq, xa_wkv, xa_wo, norm_ffn, peer_wq, peer_keys, peer_u, peer_v,
           norm_final):
    bsz, length, d = x.shape
    depth = w_in.shape[0]
    row = lambda a: a.astype(F32).reshape(1, -1)
    x2d = x.astype(F32).reshape(bsz * length, d)
    cos_t, sin_t = _rope_tables(positions)
    lb = hgrn_lb.astype(F32)
    for l in range(depth):
        proj = _inproj(x2d, row(norm_mix[l]), w_in[l].astype(BF16))
        proj3 = proj.reshape(bsz, length, IN_COLS)
        o_hg = _hgrn(proj3, lb, row(hgrn_norm[l]), l)
        bb_re, bb_im, tab = _s5_prep(s5_lambda_re[l], s5_lambda_im[l], s5_log_step[l], s5_b_re[l], s5_b_im[l])
        bbd = jnp.concatenate([_block_diag_in(bb_re), _block_diag_in(bb_im)], axis=1).astype(BF16)
        o_s5 = _s5(proj3, bbd, tab, _block_diag_out(s5_c_re[l]).astype(BF16),
                   _block_diag_out(s5_c_im[l]).astype(BF16), row(s5_d[l]),
                   s5_w_glu[l].astype(BF16), row(s5_b_glu[l]))
        o_rt = _retention(proj3, cos_t, sin_t)
        kmem, vmem = _memkv(mem.astype(F32), row(norm_mem[l]), xa_wkv[l].astype(BF16))
        x2d = _mix(x2d, proj, row(b_gate[l]), o_hg.reshape(-1, MIX), o_s5.reshape(-1, S5_WIDTH),
                   o_rt.reshape(-1, MIX), w_branch[l].astype(BF16), w_out[l].astype(BF16),
                   row(norm_cross[l]), kmem, vmem, xa_wq[l].astype(BF16), xa_wo[l].astype(BF16), length)
        keys2 = peer_keys[l].astype(BF16).reshape(PEER_HEADS * 2, PEER_NKEYS, PEER_DH)
        h, rank2, e2, n1, e1 = _route(x2d, row(norm_ffn[l]), peer_wq[l].astype(BF16).T, keys2)
        x2d = _peer(x2d, h, rank2, e2, n1, e1, peer_u[l].astype(BF16), peer_v[l].astype(BF16).T,
                    row(norm_final), final=(l == depth - 1))
    return x2d.reshape(bsz, length, d)
```

```python
import functools
import math

import jax
import jax.numpy as jnp
from jax import lax
from jax.experimental import pallas as pl
from jax.experimental.pallas import tpu as pltpu

F32 = jnp.float32
BF16 = jnp.bfloat16

D_MODEL = 1024
HEADS = 4
HEAD_DIM = 128
MIX = HEADS * HEAD_DIM
HG_F_MIN = 1e-6
S5_GROUPS = 32
S5_GROUP = 16
S5_STATE = 64
S5_WIDTH = S5_GROUPS * S5_GROUP
S5_LANES = S5_GROUPS * S5_STATE
S5_BLOCKS = 4
RET_CHUNK = 128
ROPE_BASE = 10000.0
IN_COLS = 4 * MIX + S5_WIDTH + 4 * MIX + 3 * D_MODEL
XA_HEADS = 4
XA_DH = D_MODEL // XA_HEADS
PEER_HEADS = 8
PEER_NKEYS = 128
PEER_DH = 128
PEER_TOPK = 16
EPS = 1e-6
NEG_INF = float("-inf")
UNRANKED = 99.0
BF16_ROWS = 16

VMEM_LIMIT = 56 * 1024 * 1024

NT_DIMS = (((1,), (1,)), ((), ()))
TN_DIMS = (((0,), (0,)), ((), ()))


def _params(sem):
    return pltpu.CompilerParams(dimension_semantics=sem, vmem_limit_bytes=VMEM_LIMIT)


def _rms(x, gain):
    return x * lax.rsqrt(jnp.mean(x * x, axis=-1, keepdims=True) + EPS) * gain


def _gelu(x):
    return 0.5 * x * (1.0 + jnp.tanh(math.sqrt(2.0 / math.pi) * (x + 0.044715 * (x * x * x))))


def _gelu2(x):
    c = math.sqrt(2.0 / math.pi)
    return x + x * jnp.tanh(x * (c + (c * 0.044715) * (x * x)))


def _sigmoid(x):
    return 1.0 / (1.0 + jnp.exp(-x))


def _silu(x):
    return x * _sigmoid(x)


def _dot(a, b):
    return jnp.dot(a, b, preferred_element_type=F32)


def _dot_nt(a, b):
    return lax.dot_general(a, b, NT_DIMS, preferred_element_type=F32)


def _dot_tn(a, b):
    return lax.dot_general(a, b, TN_DIMS, preferred_element_type=F32)


def _inproj_kernel(x_ref, g_ref, w_ref, o_ref, h_scr):
    @pl.when(pl.program_id(1) == 0)
    def _():
        h_scr[...] = _rms(x_ref[...], g_ref[...]).astype(BF16)

    o_ref[...] = _dot(h_scr[...], w_ref[...])


def _inproj(x2d, gain, w_bf16, tm=1024, tn=1536):
    m, d = x2d.shape
    n = w_bf16.shape[1]
    return pl.pallas_call(
        _inproj_kernel,
        grid=(m // tm, n // tn),
        in_specs=[
            pl.BlockSpec((tm, d), lambda i, j: (i, 0)),
            pl.BlockSpec((1, d), lambda i, j: (0, 0)),
            pl.BlockSpec((d, tn), lambda i, j: (0, j)),
        ],
        out_specs=pl.BlockSpec((tm, tn), lambda i, j: (i, j)),
        out_shape=jax.ShapeDtypeStruct((m, n), F32),
        scratch_shapes=[pltpu.VMEM((tm, d), BF16)],
        compiler_params=_params(("parallel", "arbitrary")),
        name="inproj",
    )(x2d, gain, w_bf16)


def _hgrn_kernel(layer, q_ref, f_ref, i_ref, g_ref, lb_ref, ng_ref, o_ref, st_scr, lvl_scr):
    tc = q_ref.shape[1]
    n_lvl = tc.bit_length() - 1

    @pl.when(pl.program_id(1) == 0)
    def _():
        st_scr[...] = jnp.zeros_like(st_scr)
        ti = lax.broadcasted_iota(jnp.int32, (tc, tc), 0)
        si = lax.broadcasted_iota(jnp.int32, (tc, tc), 1)
        top_bit = (lax.bitcast_convert_type((ti ^ si).astype(F32), jnp.int32) >> 23) - 127
        lvl_scr[...] = jnp.where(si < ti, top_bit, jnp.where(si == ti, n_lvl, -1))

    lbs = lb_ref[...]
    e = jnp.exp(lbs - jnp.max(lbs, axis=0, keepdims=True))
    sm = e / jnp.sum(e, axis=0, keepdims=True)
    lower = jnp.sum(sm[0:layer + 1], axis=0, keepdims=True) - sm[0:1]

    z = f_ref[0]
    fv = lower + (1.0 - lower) * _sigmoid(z)
    logf = jnp.log(jnp.clip(fv, HG_F_MIN, 1.0))
    kk = (1.0 - lower) * _sigmoid(-z)
    qs = q_ref[0] * HEAD_DIM ** -0.5
    vb = i_ref[0].astype(BF16)
    heads = [slice(HEAD_DIM * hh, HEAD_DIM * (hh + 1)) for hh in range(HEADS)]

    lvl = lvl_scr[...]
    qb = qs.astype(BF16)
    kb = kk.astype(BF16)
    att = [jnp.where(lvl == n_lvl, _dot_nt(qb[:, sl], kb[:, sl]), 0.0) for sl in heads]

    rowi = lax.broadcasted_iota(jnp.int32, (tc, MIX), 0)
    pre = logf
    post = jnp.zeros_like(logf)
    for level in range(n_lvl):
        h = 1 << level
        upper = (rowi & h) != 0
        x = (jnp.where(upper, qs, kk) * jnp.exp(jnp.where(upper, pre, post))).astype(BF16)
        here = lvl == level
        att = [jnp.where(here, _dot_nt(x[:, sl], x[:, sl]), a) for sl, a in zip(heads, att)]
        total = pre + post
        pre = pre + jnp.where(upper, pltpu.roll(total, h, 0), 0.0)
        post = post + jnp.where(upper, 0.0, pltpu.roll(total, tc - h, 0))

    qd = (qs * jnp.exp(pre)).astype(BF16)
    kd = (kk * jnp.exp(post)).astype(BF16)
    decay = jnp.exp(pre[tc - 1:tc, :])
    gate = g_ref[0]
    ng = ng_ref[...]
    outs = []
    for hh, sl in enumerate(heads):
        st = st_scr[hh]
        oh = _dot(att[hh].astype(BF16), vb[:, sl]) + _dot_nt(qd[:, sl], st.astype(BF16))
        st_scr[hh] = st * decay[:, sl] + _dot_tn(vb[:, sl], kd[:, sl])
        outs.append(oh * lax.rsqrt(jnp.mean(oh * oh, axis=-1, keepdims=True) + EPS) * ng[:, sl])
    o_ref[0] = (jnp.concatenate(outs, axis=1) * _silu(gate)).astype(o_ref.dtype)


def _hgrn(proj3, lb, ng, layer, tc=256):
    bsz, length, _ = proj3.shape
    tc = min(tc, length)
    col = lambda k: pl.BlockSpec((1, tc, MIX), lambda bi, ti, k=k: (bi, ti, k))
    return pl.pallas_call(
        functools.partial(_hgrn_kernel, layer),
        grid=(bsz, length // tc),
        in_specs=[col(0), col(1), col(2), col(3),
                  pl.BlockSpec(lb.shape, lambda bi, ti: (0, 0)),
                  pl.BlockSpec((1, MIX), lambda bi, ti: (0, 0))],
        out_specs=pl.BlockSpec((1, tc, MIX), lambda bi, ti: (bi, ti, 0)),
        out_shape=jax.ShapeDtypeStruct((bsz, length, MIX), BF16),
        scratch_shapes=[pltpu.VMEM((HEADS, HEAD_DIM, HEAD_DIM), F32), pltpu.VMEM((tc, tc), jnp.int32)],
        compiler_params=_params(("parallel", "arbitrary")),
        name="hgrn2",
    )(proj3, proj3, proj3, proj3, lb, ng)


def _s5_prep_kernel(lr_ref, li_ref, ls_ref, bre_ref, bim_ref, bbre_ref, bbim_ref, tab_ref):
    lr = jnp.minimum(lr_ref[...], -1e-4)
    li = li_ref[...]
    dt = jnp.exp(ls_ref[...])
    mag = jnp.exp(lr * dt)
    a_re = mag * jnp.cos(li * dt)
    a_im = mag * jnp.sin(li * dt)
    den = lr * lr + li * li
    z_re = ((a_re - 1.0) * lr + a_im * li) / den
    z_im = (a_im * lr - (a_re - 1.0) * li) / den
    bre = bre_ref[...]
    bim = bim_ref[...]
    bbre_ref[...] = z_re * bre - z_im * bim
    bbim_ref[...] = z_re * bim + z_im * bre
    row = lax.broadcasted_iota(jnp.int32, (8, a_re.shape[1]), 0)
    p_re = jnp.broadcast_to(a_re, row.shape)
    p_im = jnp.broadcast_to(a_im, row.shape)
    c_re, c_im = a_re, a_im
    powers = [(a_re, a_im)]
    for r in range(1, 8):
        c_re, c_im = c_re * a_re - c_im * a_im, c_re * a_im + c_im * a_re
        powers.append((c_re, c_im))
        p_re = jnp.where(row == r, c_re, p_re)
        p_im = jnp.where(row == r, c_im, p_im)
    for slot, d in enumerate((1, 2, 4)):
        tab_ref[2 * slot] = jnp.where(row >= d, powers[d - 1][0], 0.0)
        tab_ref[2 * slot + 1] = jnp.where(row >= d, powers[d - 1][1], 0.0)
    tab_ref[6] = p_re
    tab_ref[7] = p_im


def _s5_prep(lam_re, lam_im, log_step, b_re, b_im):
    lanes = S5_LANES
    row = lambda a: a.astype(F32).reshape(1, lanes)
    ls = jnp.repeat(log_step.astype(F32), S5_STATE).reshape(1, lanes)
    bt = lambda a: a.astype(F32).reshape(lanes, S5_GROUP).T
    full = lambda shape: pl.BlockSpec(shape, lambda: tuple(0 for _ in shape))
    return pl.pallas_call(
        _s5_prep_kernel,
        in_specs=[full((1, lanes))] * 3 + [full((S5_GROUP, lanes))] * 2,
        out_specs=[full((S5_GROUP, lanes))] * 2 + [full((8, 8, lanes))],
        out_shape=[jax.ShapeDtypeStruct((S5_GROUP, lanes), F32)] * 2
        + [jax.ShapeDtypeStruct((8, 8, lanes), F32)],
        name="s5_prep",
    )(row(lam_re), row(lam_im), ls, bt(b_re), bt(b_im))


def _s5_kernel(u_ref, bbd_ref, tab_ref, cre_ref, cim_ref, d_ref, wg_ref, bg_ref, o_ref,
               xr_scr, xi_scr, car_scr):
    tt = u_ref.shape[1]
    lanes = S5_LANES
    lw = 512

    @pl.when(pl.program_id(1) == 0)
    def _():
        car_scr[...] = jnp.zeros_like(car_scr)

    u = u_ref[0]
    ub = u.astype(BF16)
    n_blk = bbd_ref.shape[0]
    cw = S5_WIDTH // n_blk
    bl = lanes // n_blk
    for jb in range(n_blk):
        bu = _dot(ub[:, cw * jb:cw * (jb + 1)], bbd_ref[jb])
        xr_scr[:, bl * jb:bl * (jb + 1)] = bu[:, :bl]
        xi_scr[:, bl * jb:bl * (jb + 1)] = bu[:, bl:]

    for lc in range(lanes // lw):
        sl = slice(lw * lc, lw * (lc + 1))
        steps = [(d, tab_ref[2 * s][:, sl], tab_ref[2 * s + 1][:, sl]) for s, d in enumerate((1, 2, 4))]
        p_re = tab_ref[6][:, sl]
        p_im = tab_ref[7][:, sl]

        def body(k, carry, sl=sl, steps=steps, p_re=p_re, p_im=p_im):
            c_re, c_im = carry
            r0 = pl.multiple_of(k * 8, 8)
            xr = xr_scr[pl.ds(r0, 8), sl]
            xi = xi_scr[pl.ds(r0, 8), sl]
            for d, a_re, a_im in steps:
                sr = pltpu.roll(xr, d, 0)
                si = pltpu.roll(xi, d, 0)
                xr, xi = xr + a_re * sr - a_im * si, xi + a_re * si + a_im * sr
            xr, xi = xr + p_re * c_re - p_im * c_im, xi + p_re * c_im + p_im * c_re
            xr_scr[pl.ds(r0, 8), sl] = xr
            xi_scr[pl.ds(r0, 8), sl] = xi
            return xr[7:8, :], xi[7:8, :]

        c_re, c_im = lax.fori_loop(0, tt // 8, body, (car_scr[0:1, sl], car_scr[1:2, sl]), unroll=2)
        car_scr[0:1, sl] = c_re
        car_scr[1:2, sl] = c_im

    ys = []
    for jb in range(n_blk):
        cols = slice(bl * jb, bl * (jb + 1))
        ys.append(_dot(xr_scr[:, cols].astype(BF16), cre_ref[jb]) - _dot(xi_scr[:, cols].astype(BF16), cim_ref[jb]))
    y = jnp.concatenate(ys, axis=1) + d_ref[...] * u
    y = _gelu(y)
    o_ref[0] = (y * _sigmoid(_dot(y.astype(BF16), wg_ref[...]) + bg_ref[...])).astype(o_ref.dtype)


def _s5(proj3, bbd, tab, cre, cim, dskip, wg, bg, tt=256):
    bsz, length, _ = proj3.shape
    lanes = S5_LANES
    const = lambda shape: pl.BlockSpec(shape, lambda bi, ti: tuple(0 for _ in shape))
    return pl.pallas_call(
        _s5_kernel,
        grid=(bsz, length // tt),
        in_specs=[pl.BlockSpec((1, tt, S5_WIDTH), lambda bi, ti: (bi, ti, 4)),
                  const(bbd.shape), const((8, 8, lanes)), const(cre.shape), const(cim.shape),
                  const((1, S5_WIDTH)), const((S5_WIDTH, S5_WIDTH)), const((1, S5_WIDTH))],
        out_specs=pl.BlockSpec((1, tt, S5_WIDTH), lambda bi, ti: (bi, ti, 0)),
        out_shape=jax.ShapeDtypeStruct((bsz, length, S5_WIDTH), BF16),
        scratch_shapes=[pltpu.VMEM((tt, lanes), F32), pltpu.VMEM((tt, lanes), F32),
                        pltpu.VMEM((8, lanes), F32)],
        compiler_params=_params(("parallel", "arbitrary")),
        name="s5",
    )(proj3, bbd, tab, cre, cim, dskip, wg, bg)


def _block_diag_in(bb_t):
    g, c, p = S5_GROUPS, S5_GROUP, S5_STATE
    gb = g // S5_BLOCKS
    per_group = bb_t.reshape(c, S5_BLOCKS, gb, p).transpose(1, 2, 0, 3)
    eye = jnp.eye(gb, dtype=F32)
    full = eye[None, :, None, :, None] * per_group[:, :, :, None, :]
    return full.reshape(S5_BLOCKS, gb * c, gb * p)


def _block_diag_out(cmat):
    g, c, p = S5_GROUPS, S5_GROUP, S5_STATE
    gb = g // S5_BLOCKS
    per_group = cmat.astype(F32).reshape(S5_BLOCKS, gb, c, p).transpose(0, 1, 3, 2)
    eye = jnp.eye(gb, dtype=F32)
    full = eye[None, :, None, :, None] * per_group[:, :, :, None, :]
    return full.reshape(S5_BLOCKS, gb * p, gb * c)


def _rope_kernel(pos_ref, cos_ref, sin_ref):
    pos = pos_ref[0].astype(F32)
    half = HEAD_DIM // 2
    lane = lax.broadcasted_iota(jnp.int32, (1, HEAD_DIM), 1)
    j = (lane & (half - 1)).astype(F32)
    inv = jnp.exp(j * (-math.log(ROPE_BASE) / half))
    ang = pos * inv
    cos_ref[0] = jnp.cos(ang)
    sin_ref[0] = jnp.where(lane < half, -1.0, 1.0) * jnp.sin(ang)


def _rope_tables(positions, tr=1024):
    bsz, length = positions.shape
    tr = min(tr, length)
    spec_o = pl.BlockSpec((1, tr, HEAD_DIM), lambda bi, ti: (bi, ti, 0))
    return pl.pallas_call(
        _rope_kernel,
        grid=(bsz, length // tr),
        in_specs=[pl.BlockSpec((1, tr, 1), lambda bi, ti: (bi, ti, 0))],
        out_specs=[spec_o, spec_o],
        out_shape=[jax.ShapeDtypeStruct((bsz, length, HEAD_DIM), F32)] * 2,
        compiler_params=_params(("parallel", "parallel")),
        name="rope_tables",
    )(positions.reshape(bsz, length, 1))


def _ret_kernel(q_ref, k_ref, v_ref, g_ref, cos_ref, sin_ref, o_ref, s_scr):
    c = RET_CHUNK

    @pl.when(pl.program_id(1) == 0)
    def _():
        s_scr[...] = jnp.zeros_like(s_scr)

    cosf = cos_ref[0]
    sinf = sin_ref[0]
    rel = (lax.broadcasted_iota(jnp.int32, (c, c), 0) - lax.broadcasted_iota(jnp.int32, (c, c), 1)).astype(F32)
    idx = lax.broadcasted_iota(jnp.int32, (c, 1), 0).astype(F32)
    qa, ka, va, ga = q_ref[0], k_ref[0], v_ref[0], g_ref[0]
    outs = []
    for hh in range(HEADS):
        sl = slice(HEAD_DIM * hh, HEAD_DIM * (hh + 1))
        lg = math.log1p(-2.0 ** (-5.0 - hh))
        q = qa[:, sl]
        k = ka[:, sl]
        vb = va[:, sl].astype(BF16)
        qh = q * cosf + pltpu.roll(q, HEAD_DIM // 2, 1) * sinf
        kh = (k * cosf + pltpu.roll(k, HEAD_DIM // 2, 1) * sinf) * HEAD_DIM ** -0.5
        dmat = jnp.where(rel >= 0.0, jnp.exp(lg * jnp.maximum(rel, 0.0)), 0.0)
        scores = _dot_nt(qh.astype(BF16), kh.astype(BF16)) * dmat
        o = _dot(scores.astype(BF16), vb)
        q_dec = qh * jnp.exp(lg * (idx + 1.0))
        k_dec = kh * jnp.exp(lg * (c - 1.0 - idx))
        st = s_scr[hh]
        o = o + _dot(q_dec.astype(BF16), st.astype(BF16))
        s_scr[hh] = math.exp(lg * c) * st + _dot_tn(k_dec.astype(BF16), vb)
        mu = jnp.mean(o, axis=-1, keepdims=True)
        var = jnp.mean((o - mu) ** 2, axis=-1, keepdims=True)
        outs.append((o - mu) * lax.rsqrt(var + EPS))
    o_ref[0] = (jnp.concatenate(outs, axis=1) * _silu(ga)).astype(o_ref.dtype)


def _retention(proj3, cos_t, sin_t):
    bsz, length, _ = proj3.shape
    c = RET_CHUNK
    col = lambda k: pl.BlockSpec((1, c, MIX), lambda bi, ti, k=k: (bi, ti, k))
    tab = pl.BlockSpec((1, c, HEAD_DIM), lambda bi, ti: (bi, ti, 0))
    return pl.pallas_call(
        _ret_kernel,
        grid=(bsz, length // c),
        in_specs=[col(5), col(6), col(7), col(8), tab, tab],
        out_specs=pl.BlockSpec((1, c, MIX), lambda bi, ti: (bi, ti, 0)),
        out_shape=jax.ShapeDtypeStruct((bsz, length, MIX), BF16),
        scratch_shapes=[pltpu.VMEM((HEADS, HEAD_DIM, HEAD_DIM), F32)],
        compiler_params=_params(("parallel", "arbitrary")),
        name="retention",
    )(proj3, proj3, proj3, proj3, cos_t, sin_t)


def _memkv_kernel(m_ref, g_ref, w_ref, k_ref, v_ref):
    kv = _dot(_rms(m_ref[0], g_ref[...]).astype(BF16), w_ref[...])
    k_ref[0] = kv[:, :D_MODEL].astype(BF16)
    v_ref[0] = kv[:, D_MODEL:].astype(BF16)


def _memkv(mem, gain, wkv_bf16):
    bsz, m, d = mem.shape
    spec_o = pl.BlockSpec((1, m, d), lambda bi: (bi, 0, 0))
    return pl.pallas_call(
        _memkv_kernel,
        grid=(bsz,),
        in_specs=[pl.BlockSpec((1, m, d), lambda bi: (bi, 0, 0)),
                  pl.BlockSpec((1, d), lambda bi: (0, 0)),
                  pl.BlockSpec((d, 2 * d), lambda bi: (0, 0))],
        out_specs=[spec_o, spec_o],
        out_shape=[jax.ShapeDtypeStruct((bsz, m, d), BF16)] * 2,
        compiler_params=_params(("parallel",)),
        name="mem_kv",
    )(mem, gain, wkv_bf16)


def _mix_kernel(x_ref, ga_ref, gb_ref, bg_ref, hg_ref, s5_ref, rt_ref, wb_ref, wout_ref,
                nc_ref, k_ref, v_ref, wq_ref, wo_ref, o_ref):
    d = D_MODEL
    logits = jnp.concatenate([ga_ref[...], gb_ref[...]], axis=1) + bg_ref[...]
    gates = _sigmoid(logits)
    merged = (gates[:, 0:d] * _dot(hg_ref[...], wb_ref[0:MIX, :])
              + gates[:, d:2 * d] * _dot(s5_ref[...], wb_ref[MIX:MIX + S5_WIDTH, :])
              + gates[:, 2 * d:3 * d] * _dot(rt_ref[...], wb_ref[MIX + S5_WIDTH:, :]))
    x1 = x_ref[...] + _dot(merged.astype(BF16), wout_ref[...])

    hc = _rms(x1, nc_ref[...]).astype(BF16)
    q = _dot(hc, wq_ref[...])
    kk = k_ref[0]
    vv = v_ref[0]
    outs = []
    for hh in range(XA_HEADS):
        sl = slice(XA_DH * hh, XA_DH * (hh + 1))
        s = _dot_nt(q[:, sl].astype(BF16), kk[:, sl]) * XA_DH ** -0.5
        p = jnp.exp(s - jnp.max(s, axis=-1, keepdims=True))
        p = p / jnp.sum(p, axis=-1, keepdims=True)
        outs.append(_dot(p.astype(BF16), vv[:, sl]))
    attn = jnp.concatenate(outs, axis=1)
    o_ref[...] = x1 + _dot(attn.astype(BF16), wo_ref[...])


def _mix(x2d, proj, bgate, o_hg, o_s5, o_rt, wb, wout, ncross, kmem, vmem, wq, wo, length, tm=512):
    m, d = x2d.shape
    per_batch = length // tm
    tok = lambda w: pl.BlockSpec((tm, w), lambda i: (i, 0))
    const = lambda shape: pl.BlockSpec(shape, lambda i: tuple(0 for _ in shape))
    gw = 1536
    mem_spec = pl.BlockSpec((1,) + kmem.shape[1:], lambda i: (i // per_batch, 0, 0))
    return pl.pallas_call(
        _mix_kernel,
        grid=(m // tm,),
        in_specs=[tok(d),
                  pl.BlockSpec((tm, gw), lambda i: (i, 3)), pl.BlockSpec((tm, gw), lambda i: (i, 4)),
                  const((1, 3 * d)), tok(MIX), tok(S5_WIDTH), tok(MIX),
                  const(wb.shape), const((d, d)), const((1, d)), mem_spec, mem_spec,
                  const((d, d)), const((d, d))],
        out_specs=tok(d),
        out_shape=jax.ShapeDtypeStruct((m, d), F32),
        compiler_params=_params(("parallel",)),
        name="merge_xattn",
    )(x2d, proj, proj, bgate, o_hg, o_s5, o_rt, wb, wout, ncross, kmem, vmem, wq, wo)


def _top_extract(s, with_rank):
    t = s.shape[1]
    row16 = lax.broadcasted_iota(jnp.int32, (PEER_TOPK, t), 0)
    vals = jnp.zeros((PEER_TOPK, t), F32)
    rank = jnp.full(s.shape, UNRANKED, F32) if with_rank else None
    w = s
    for r in range(PEER_TOPK):
        m = jnp.max(w, axis=0, keepdims=True)
        hit = w == m
        if with_rank:
            rank = jnp.where(hit, float(r + 1), rank)
        w = jnp.where(hit, NEG_INF, w)
        vals = jnp.where(row16 == r, m, vals)
    return vals, rank


def _route_kernel(x_ref, g_ref, wqt_ref, keys_ref, h_ref, rank2_ref, e2_ref, n1_ref, e1_ref):
    t = x_ref.shape[0]
    k = PEER_TOPK
    hb = _rms(x_ref[...], g_ref[...]).astype(BF16)
    h_ref[...] = hb
    qt = _dot_nt(wqt_ref[...], hb).astype(BF16)
    row16 = lax.broadcasted_iota(jnp.int32, (k, t), 0)
    row8 = lax.broadcasted_iota(jnp.int32, (8, t), 0)
    zeros8 = jnp.zeros((8, t), F32)
    for hd in range(PEER_HEADS):
        base = hd * 2 * PEER_DH
        s1 = _dot(keys_ref[2 * hd], qt[base:base + PEER_DH])
        s2 = _dot(keys_ref[2 * hd + 1], qt[base + PEER_DH:base + 2 * PEER_DH])
        a, _ = _top_extract(s1, False)
        b, rank2 = _top_extract(s2, True)
        pieces = [a + b[0:1]]
        for j in range(2, 9):
            pieces.append(jnp.where(row8 < k // j, a[0:8] + b[j - 1:j], NEG_INF))
        pieces.append(a[0:1] + b[8:16])
        cand = jnp.concatenate(pieces, axis=0)
        top = a[0:1] + b[0:1]
        work = cand
        thr = top
        for r in range(k):
            thr = jnp.max(work, axis=0, keepdims=True)
            work = jnp.where(work == thr, NEG_INF, work)
        sel = cand >= thr
        zsum = jnp.sum(jnp.where(sel, jnp.exp(cand - top), 0.0), axis=0, keepdims=True)
        picked = sel.astype(F32)
        low = picked[16:24]
        for j in range(3, 9):
            low = low + picked[8 * j:8 * j + 8]
        cnt_tail = jnp.sum(picked[72:80], axis=0, keepdims=True)
        n_of_i = picked[0:16] + jnp.concatenate([low, zeros8], axis=0) + jnp.where(row16 == 0, cnt_tail, 0.0)
        n1 = jnp.zeros(s1.shape, F32)
        for i in range(k):
            n1 = jnp.where(s1 == a[i:i + 1], n_of_i[i:i + 1], n1)
        rows = slice(hd * PEER_NKEYS, (hd + 1) * PEER_NKEYS)
        rank2_ref[rows, :] = rank2.astype(BF16)
        e2_ref[rows, :] = (0.5 * jnp.exp(s2 - b[0:1]) / zsum).astype(BF16)
        n1_ref[rows, :] = n1
        e1_ref[rows, :] = jnp.exp(s1 - a[0:1])


def _route(x2d, gain, wqt, keys2, tt=256):
    m, d = x2d.shape
    rows = PEER_HEADS * PEER_NKEYS
    tab = pl.BlockSpec((rows, tt), lambda i: (0, i))
    return pl.pallas_call(
        _route_kernel,
        grid=(m // tt,),
        in_specs=[pl.BlockSpec((tt, d), lambda i: (i, 0)),
                  pl.BlockSpec((1, d), lambda i: (0, 0)),
                  pl.BlockSpec(wqt.shape, lambda i: (0, 0)),
                  pl.BlockSpec(keys2.shape, lambda i: (0, 0, 0))],
        out_specs=[pl.BlockSpec((tt, d), lambda i: (i, 0)), tab, tab, tab, tab],
        out_shape=[jax.ShapeDtypeStruct((m, d), BF16)] + [jax.ShapeDtypeStruct((rows, m), BF16)] * 2
        + [jax.ShapeDtypeStruct((rows, m), F32)] * 2,
        compiler_params=_params(("parallel",)),
        name="peer_route",
    )(x2d, gain, wqt, keys2)


def _peer_kernel(final, x_ref, h_ref, rank2_ref, e2_ref, n1_ref, e1_ref, u_ref, vt_ref, gf_ref,
                 o_ref, acc_scr):
    j = pl.program_id(1)
    eb = u_ref.shape[0]
    nk = PEER_NKEYS
    chunk = 512
    per_chunk = chunk // nk

    @pl.when(j == 0)
    def _():
        acc_scr[...] = jnp.zeros_like(acc_scr)

    tt = h_ref.shape[0]
    half = tt // 2
    i1_base = pl.multiple_of(j * (eb // nk), eb // nk)
    n_chunks = eb // chunk
    assert per_chunk == 4

    def score(ck, th):
        return _dot_nt(u_ref[ck * chunk:(ck + 1) * chunk, :], h_ref[th * half:(th + 1) * half, :])

    def key_row(ref, hd, il, cols):
        r = ref[pl.ds(hd * nk + i1_base + il, 1), :][:, cols]
        return jnp.tile(jnp.broadcast_to(r, (BF16_ROWS, half)).astype(BF16), (nk // BF16_ROWS, 1))

    def act_tile(z_half, ck, sub, th):
        il = ck * per_chunk + sub
        cols = slice(th * half, (th + 1) * half)
        gate = None
        for hd in range(PEER_HEADS):
            rows = slice(hd * nk, (hd + 1) * nk)
            sel = jnp.where(rank2_ref[rows, cols] <= key_row(n1_ref, hd, il, cols), e2_ref[rows, cols], 0.0)
            term = sel * key_row(e1_ref, hd, il, cols)
            gate = term if gate is None else gate + term
        return _gelu2(z_half[sub * nk:(sub + 1) * nk, :]).astype(BF16) * gate

    def apply(ck, th, act):
        cols = slice(th * half, (th + 1) * half)
        acc_scr[:, cols] += _dot(vt_ref[:, ck * chunk:(ck + 1) * chunk], act)

    z = [score(0, 0), score(0, 1)]
    prev = None
    for ck in range(n_chunks):
        acts = ([], [])
        z_next = [None, None]
        for sub in range(per_chunk):
            for th in range(2):
                acts[th].append(act_tile(z[th], ck, sub, th))
            if sub < 2 and ck + 1 < n_chunks:
                z_next[sub] = score(ck + 1, sub)
            if sub >= 2 and prev is not None:
                apply(ck - 1, sub - 2, prev[sub - 2])
        prev = [jnp.concatenate(acts[0], axis=0), jnp.concatenate(acts[1], axis=0)]
        z = z_next
    apply(n_chunks - 1, 0, prev[0])
    apply(n_chunks - 1, 1, prev[1])

    @pl.when(j == pl.num_programs(1) - 1)
    def _():
        y = x_ref[...] + acc_scr[...].T
        if final:
            y = _rms(y, gf_ref[...])
        o_ref[...] = y


def _peer(x2d, h, rank2, e2, n1, e1, u_bf16, vt_bf16, gfinal, final, tt=512, eb=2048):
    m, d = x2d.shape
    ne = u_bf16.shape[0]
    rows = PEER_HEADS * PEER_NKEYS
    tab = pl.BlockSpec((rows, tt), lambda i, j: (0, i))
    return pl.pallas_call(
        functools.partial(_peer_kernel, final),
        grid=(m // tt, ne // eb),
        in_specs=[pl.BlockSpec((tt, d), lambda i, j: (i, 0)),
                  pl.BlockSpec((tt, d), lambda i, j: (i, 0)),
                  tab, tab, tab, tab,
                  pl.BlockSpec((eb, d), lambda i, j: (j, 0)),
                  pl.BlockSpec((d, eb), lambda i, j: (0, j)),
                  pl.BlockSpec((1, d), lambda i, j: (0, 0))],
        out_specs=pl.BlockSpec((tt, d), lambda i, j: (i, 0)),
        out_shape=jax.ShapeDtypeStruct((m, d), F32),
        scratch_shapes=[pltpu.VMEM((d, tt), F32)],
        compiler_params=_params(("parallel", "arbitrary")),
        name="peer_experts",
    )(x2d, h, rank2, e2, n1, e1, u_bf16, vt_bf16, gfinal)


def kernel(x, mem, positions, norm_mix, w_in, b_gate, hgrn_lb, hgrn_norm, s5_lambda_re, s5_lambda_im,
           s5_log_step, s5_b_re, s5_b_im, s5_c_re, s5_c_im, s5_d, s5_w_glu, s5_b_glu, w_branch, w_out,
           norm_cross, norm_mem, xa_wq, xa_wkv, xa_wo, norm_ffn, peer_wq, peer_keys, peer_u, peer_v,
           norm_final):
    bsz, length, d = x.shape
    depth = w_in.shape[0]
    row = lambda a: a.astype(F32).reshape(1, -1)
    x2d = x.astype(F32).reshape(bsz * length, d)
    cos_t, sin_t = _rope_tables(positions)
    lb = hgrn_lb.astype(F32)
    for l in range(depth):
        proj = _inproj(x2d, row(norm_mix[l]), w_in[l].astype(BF16))
        proj3 = proj.reshape(bsz, length, IN_COLS)
        o_hg = _hgrn(proj3, lb, row(hgrn_norm[l]), l)
        bb_re, bb_im, tab = _s5_prep(s5_lambda_re[l], s5_lambda_im[l], s5_log_step[l], s5_b_re[l], s5_b_im[l])
        bbd = jnp.concatenate([_block_diag_in(bb_re), _block_diag_in(bb_im)], axis=2).astype(BF16)
        o_s5 = _s5(proj3, bbd, tab, _block_diag_out(s5_c_re[l]).astype(BF16),
                   _block_diag_out(s5_c_im[l]).astype(BF16), row(s5_d[l]),
                   s5_w_glu[l].astype(BF16), row(s5_b_glu[l]))
        o_rt = _retention(proj3, cos_t, sin_t)
        kmem, vmem = _memkv(mem.astype(F32), row(norm_mem[l]), xa_wkv[l].astype(BF16))
        x2d = _mix(x2d, proj, row(b_gate[l]), o_hg.reshape(-1, MIX), o_s5.reshape(-1, S5_WIDTH),
                   o_rt.reshape(-1, MIX), w_branch[l].astype(BF16), w_out[l].astype(BF16),
                   row(norm_cross[l]), kmem, vmem, xa_wq[l].astype(BF16), xa_wo[l].astype(BF16), length)
        keys2 = peer_keys[l].astype(BF16).reshape(PEER_HEADS * 2, PEER_NKEYS, PEER_DH)
        h, rank2, e2, n1, e1 = _route(x2d, row(norm_ffn[l]), peer_wq[l].astype(BF16).T, keys2)
        x2d = _peer(x2d, h, rank2, e2, n1, e1, peer_u[l].astype(BF16), peer_v[l].astype(BF16).T,
                    row(norm_final), final=(l == depth - 1))
    return x2d.reshape(bsz, length, d)
```

```python
import functools
import math

import jax
import jax.numpy as jnp
from jax import lax
from jax.experimental import pallas as pl
from jax.experimental.pallas import tpu as pltpu

F32 = jnp.float32
BF16 = jnp.bfloat16

D_MODEL = 1024
HEADS = 4
HEAD_DIM = 128
MIX = HEADS * HEAD_DIM
HG_F_MIN = 1e-6
S5_GROUPS = 32
S5_GROUP = 16
S5_STATE = 64
S5_WIDTH = S5_GROUPS * S5_GROUP
S5_LANES = S5_GROUPS * S5_STATE
S5_BLOCKS = 4
RET_CHUNK = 256
ROPE_BASE = 10000.0
IN_COLS = 4 * MIX + S5_WIDTH + 4 * MIX + 3 * D_MODEL
XA_HEADS = 4
XA_DH = D_MODEL // XA_HEADS
PEER_HEADS = 8
PEER_NKEYS = 128
PEER_DH = 128
PEER_TOPK = 16
EPS = 1e-6
NEG_INF = float("-inf")
UNRANKED = 99.0
BF16_ROWS = 16
PEER_CHUNK_BLOCKS = (4, 4, 4, 4)

VMEM_LIMIT = 56 * 1024 * 1024

NT_DIMS = (((1,), (1,)), ((), ()))
TN_DIMS = (((0,), (0,)), ((), ()))


def _params(sem):
    return pltpu.CompilerParams(dimension_semantics=sem, vmem_limit_bytes=VMEM_LIMIT)


def _rms(x, gain):
    return x * lax.rsqrt(jnp.mean(x * x, axis=-1, keepdims=True) + EPS) * gain


def _gelu(x):
    return 0.5 * x * (1.0 + jnp.tanh(math.sqrt(2.0 / math.pi) * (x + 0.044715 * (x * x * x))))


def _gelu2(x):
    c = math.sqrt(2.0 / math.pi)
    return x + x * jnp.tanh(x * (c + (c * 0.044715) * (x * x)))


def _sigmoid(x):
    return 1.0 / (1.0 + jnp.exp(-x))


def _silu(x):
    return x * _sigmoid(x)


def _dot(a, b):
    return jnp.dot(a, b, preferred_element_type=F32)


def _dot_nt(a, b):
    return lax.dot_general(a, b, NT_DIMS, preferred_element_type=F32)


def _dot_tn(a, b):
    return lax.dot_general(a, b, TN_DIMS, preferred_element_type=F32)


def _inproj_kernel(x_ref, g_ref, w_ref, o_ref, h_scr):
    @pl.when(pl.program_id(1) == 0)
    def _():
        h_scr[...] = _rms(x_ref[...], g_ref[...]).astype(BF16)

    o_ref[...] = _dot(h_scr[...], w_ref[...])


def _inproj(x2d, gain, w_bf16, tm=1024, tn=2560):
    m, d = x2d.shape
    n = w_bf16.shape[1]
    return pl.pallas_call(
        _inproj_kernel,
        grid=(m // tm, n // tn),
        in_specs=[
            pl.BlockSpec((tm, d), lambda i, j: (i, 0)),
            pl.BlockSpec((1, d), lambda i, j: (0, 0)),
            pl.BlockSpec((d, tn), lambda i, j: (0, j)),
        ],
        out_specs=pl.BlockSpec((tm, tn), lambda i, j: (i, j)),
        out_shape=jax.ShapeDtypeStruct((m, n), F32),
        scratch_shapes=[pltpu.VMEM((tm, d), BF16)],
        compiler_params=_params(("parallel", "arbitrary")),
        name="inproj",
    )(x2d, gain, w_bf16)


def _hgrn_kernel(layer, q_ref, f_ref, i_ref, g_ref, lb_ref, ng_ref, o_ref, st_scr, lvl_scr):
    tc = q_ref.shape[1]
    n_lvl = tc.bit_length() - 1

    @pl.when(pl.program_id(1) == 0)
    def _():
        st_scr[...] = jnp.zeros_like(st_scr)
        ti = lax.broadcasted_iota(jnp.int32, (tc, tc), 0)
        si = lax.broadcasted_iota(jnp.int32, (tc, tc), 1)
        top_bit = (lax.bitcast_convert_type((ti ^ si).astype(F32), jnp.int32) >> 23) - 127
        lvl_scr[...] = jnp.where(si < ti, top_bit, jnp.where(si == ti, n_lvl, -1))

    lbs = lb_ref[...]
    e = jnp.exp(lbs - jnp.max(lbs, axis=0, keepdims=True))
    sm = e / jnp.sum(e, axis=0, keepdims=True)
    lower = jnp.sum(sm[0:layer + 1], axis=0, keepdims=True) - sm[0:1]

    z = f_ref[0]
    fv = lower + (1.0 - lower) * _sigmoid(z)
    logf = jnp.log(jnp.clip(fv, HG_F_MIN, 1.0))
    kk = (1.0 - lower) * _sigmoid(-z)
    qs = q_ref[0] * HEAD_DIM ** -0.5
    vb = i_ref[0].astype(BF16)
    heads = [slice(HEAD_DIM * hh, HEAD_DIM * (hh + 1)) for hh in range(HEADS)]

    lvl = lvl_scr[...]
    qb = qs.astype(BF16)
    kb = kk.astype(BF16)
    att = [jnp.where(lvl == n_lvl, _dot_nt(qb[:, sl], kb[:, sl]), 0.0) for sl in heads]

    rowi = lax.broadcasted_iota(jnp.int32, (tc, MIX), 0)
    pre = logf
    post = jnp.zeros_like(logf)
    for level in range(n_lvl):
        h = 1 << level
        upper = (rowi & h) != 0
        x = (jnp.where(upper, qs, kk) * jnp.exp(jnp.where(upper, pre, post))).astype(BF16)
        here = lvl == level
        att = [jnp.where(here, _dot_nt(x[:, sl], x[:, sl]), a) for sl, a in zip(heads, att)]
        total = pre + post
        pre = pre + jnp.where(upper, pltpu.roll(total, h, 0), 0.0)
        post = post + jnp.where(upper, 0.0, pltpu.roll(total, tc - h, 0))

    qd = (qs * jnp.exp(pre)).astype(BF16)
    kd = (kk * jnp.exp(post)).astype(BF16)
    decay = jnp.exp(pre[tc - 1:tc, :])
    gate = g_ref[0]
    ng = ng_ref[...]
    outs = []
    for hh, sl in enumerate(heads):
        st = st_scr[hh]
        oh = _dot(att[hh].astype(BF16), vb[:, sl]) + _dot_nt(qd[:, sl], st.astype(BF16))
        st_scr[hh] = st * decay[:, sl] + _dot_tn(vb[:, sl], kd[:, sl])
        outs.append(oh * lax.rsqrt(jnp.mean(oh * oh, axis=-1, keepdims=True) + EPS) * ng[:, sl])
    o_ref[0] = (jnp.concatenate(outs, axis=1) * _silu(gate)).astype(o_ref.dtype)


def _hgrn(proj3, lb, ng, layer, tc=256):
    bsz, length, _ = proj3.shape
    tc = min(tc, length)
    col = lambda k: pl.BlockSpec((1, tc, MIX), lambda bi, ti, k=k: (bi, ti, k))
    return pl.pallas_call(
        functools.partial(_hgrn_kernel, layer),
        grid=(bsz, length // tc),
        in_specs=[col(0), col(1), col(2), col(3),
                  pl.BlockSpec(lb.shape, lambda bi, ti: (0, 0)),
                  pl.BlockSpec((1, MIX), lambda bi, ti: (0, 0))],
        out_specs=pl.BlockSpec((1, tc, MIX), lambda bi, ti: (bi, ti, 0)),
        out_shape=jax.ShapeDtypeStruct((bsz, length, MIX), BF16),
        scratch_shapes=[pltpu.VMEM((HEADS, HEAD_DIM, HEAD_DIM), F32), pltpu.VMEM((tc, tc), jnp.int32)],
        compiler_params=_params(("parallel", "arbitrary")),
        name="hgrn2",
    )(proj3, proj3, proj3, proj3, lb, ng)


def _s5_prep_kernel(lr_ref, li_ref, ls_ref, bre_ref, bim_ref, bbre_ref, bbim_ref, tab_ref):
    lr = jnp.minimum(lr_ref[...], -1e-4)
    li = li_ref[...]
    dt = jnp.exp(ls_ref[...])
    mag = jnp.exp(lr * dt)
    a_re = mag * jnp.cos(li * dt)
    a_im = mag * jnp.sin(li * dt)
    den = lr * lr + li * li
    z_re = ((a_re - 1.0) * lr + a_im * li) / den
    z_im = (a_im * lr - (a_re - 1.0) * li) / den
    bre = bre_ref[...]
    bim = bim_ref[...]
    bbre_ref[...] = z_re * bre - z_im * bim
    bbim_ref[...] = z_re * bim + z_im * bre
    row = lax.broadcasted_iota(jnp.int32, (8, a_re.shape[1]), 0)
    p_re = jnp.broadcast_to(a_re, row.shape)
    p_im = jnp.broadcast_to(a_im, row.shape)
    c_re, c_im = a_re, a_im
    powers = [(a_re, a_im)]
    for r in range(1, 8):
        c_re, c_im = c_re * a_re - c_im * a_im, c_re * a_im + c_im * a_re
        powers.append((c_re, c_im))
        p_re = jnp.where(row == r, c_re, p_re)
        p_im = jnp.where(row == r, c_im, p_im)
    for slot, d in enumerate((1, 2, 4)):
        tab_ref[2 * slot] = jnp.where(row >= d, powers[d - 1][0], 0.0)
        tab_ref[2 * slot + 1] = jnp.where(row >= d, powers[d - 1][1], 0.0)
    tab_ref[6] = p_re
    tab_ref[7] = p_im


def _s5_prep(lam_re, lam_im, log_step, b_re, b_im):
    lanes = S5_LANES
    row = lambda a: a.astype(F32).reshape(1, lanes)
    ls = jnp.repeat(log_step.astype(F32), S5_STATE).reshape(1, lanes)
    bt = lambda a: a.astype(F32).reshape(lanes, S5_GROUP).T
    full = lambda shape: pl.BlockSpec(shape, lambda: tuple(0 for _ in shape))
    return pl.pallas_call(
        _s5_prep_kernel,
        in_specs=[full((1, lanes))] * 3 + [full((S5_GROUP, lanes))] * 2,
        out_specs=[full((S5_GROUP, lanes))] * 2 + [full((8, 8, lanes))],
        out_shape=[jax.ShapeDtypeStruct((S5_GROUP, lanes), F32)] * 2
        + [jax.ShapeDtypeStruct((8, 8, lanes), F32)],
        name="s5_prep",
    )(row(lam_re), row(lam_im), ls, bt(b_re), bt(b_im))


def _s5_kernel(u_ref, bbd_ref, tab_ref, cre_ref, cim_ref, d_ref, wg_ref, bg_ref, o_ref,
               xr_scr, xi_scr, car_scr):
    tt = u_ref.shape[1]
    lanes = S5_LANES
    lw = 512

    @pl.when(pl.program_id(1) == 0)
    def _():
        car_scr[...] = jnp.zeros_like(car_scr)

    u = u_ref[0]
    ub = u.astype(BF16)
    n_blk = bbd_ref.shape[0]
    cw = S5_WIDTH // n_blk
    bl = lanes // n_blk
    for jb in range(n_blk):
        bu = _dot(ub[:, cw * jb:cw * (jb + 1)], bbd_ref[jb])
        xr_scr[:, bl * jb:bl * (jb + 1)] = bu[:, :bl]
        xi_scr[:, bl * jb:bl * (jb + 1)] = bu[:, bl:]

    for lc in range(lanes // lw):
        sl = slice(lw * lc, lw * (lc + 1))
        steps = [(d, tab_ref[2 * s][:, sl], tab_ref[2 * s + 1][:, sl]) for s, d in enumerate((1, 2, 4))]
        p_re = tab_ref[6][:, sl]
        p_im = tab_ref[7][:, sl]

        def body(k, carry, sl=sl, steps=steps, p_re=p_re, p_im=p_im):
            c_re, c_im = carry
            r0 = pl.multiple_of(k * 8, 8)
            xr = xr_scr[pl.ds(r0, 8), sl]
            xi = xi_scr[pl.ds(r0, 8), sl]
            for d, a_re, a_im in steps:
                sr = pltpu.roll(xr, d, 0)
                si = pltpu.roll(xi, d, 0)
                xr, xi = xr + a_re * sr - a_im * si, xi + a_re * si + a_im * sr
            xr, xi = xr + p_re * c_re - p_im * c_im, xi + p_re * c_im + p_im * c_re
            xr_scr[pl.ds(r0, 8), sl] = xr
            xi_scr[pl.ds(r0, 8), sl] = xi
            return xr[7:8, :], xi[7:8, :]

        c_re, c_im = lax.fori_loop(0, tt // 8, body, (car_scr[0:1, sl], car_scr[1:2, sl]), unroll=2)
        car_scr[0:1, sl] = c_re
        car_scr[1:2, sl] = c_im

    ys = []
    for jb in range(n_blk):
        cols = slice(bl * jb, bl * (jb + 1))
        ys.append(_dot(xr_scr[:, cols].astype(BF16), cre_ref[jb]) - _dot(xi_scr[:, cols].astype(BF16), cim_ref[jb]))
    y = jnp.concatenate(ys, axis=1) + d_ref[...] * u
    y = _gelu(y)
    o_ref[0] = (y * _sigmoid(_dot(y.astype(BF16), wg_ref[...]) + bg_ref[...])).astype(o_ref.dtype)


def _s5(proj3, bbd, tab, cre, cim, dskip, wg, bg, tt=256):
    bsz, length, _ = proj3.shape
    lanes = S5_LANES
    const = lambda shape: pl.BlockSpec(shape, lambda bi, ti: tuple(0 for _ in shape))
    return pl.pallas_call(
        _s5_kernel,
        grid=(bsz, length // tt),
        in_specs=[pl.BlockSpec((1, tt, S5_WIDTH), lambda bi, ti: (bi, ti, 4)),
                  const(bbd.shape), const((8, 8, lanes)), const(cre.shape), const(cim.shape),
                  const((1, S5_WIDTH)), const((S5_WIDTH, S5_WIDTH)), const((1, S5_WIDTH))],
        out_specs=pl.BlockSpec((1, tt, S5_WIDTH), lambda bi, ti: (bi, ti, 0)),
        out_shape=jax.ShapeDtypeStruct((bsz, length, S5_WIDTH), BF16),
        scratch_shapes=[pltpu.VMEM((tt, lanes), F32), pltpu.VMEM((tt, lanes), F32),
                        pltpu.VMEM((8, lanes), F32)],
        compiler_params=_params(("parallel", "arbitrary")),
        name="s5",
    )(proj3, bbd, tab, cre, cim, dskip, wg, bg)


def _block_diag_in(bb_t):
    g, c, p = S5_GROUPS, S5_GROUP, S5_STATE
    gb = g // S5_BLOCKS
    per_group = bb_t.reshape(c, S5_BLOCKS, gb, p).transpose(1, 2, 0, 3)
    eye = jnp.eye(gb, dtype=F32)
    full = eye[None, :, None, :, None] * per_group[:, :, :, None, :]
    return full.reshape(S5_BLOCKS, gb * c, gb * p)


def _block_diag_out(cmat):
    g, c, p = S5_GROUPS, S5_GROUP, S5_STATE
    gb = g // S5_BLOCKS
    per_group = cmat.astype(F32).reshape(S5_BLOCKS, gb, c, p).transpose(0, 1, 3, 2)
    eye = jnp.eye(gb, dtype=F32)
    full = eye[None, :, None, :, None] * per_group[:, :, :, None, :]
    return full.reshape(S5_BLOCKS, gb * p, gb * c)


def _rope_kernel(pos_ref, cos_ref, sin_ref):
    pos = pos_ref[0].astype(F32)
    half = HEAD_DIM // 2
    lane = lax.broadcasted_iota(jnp.int32, (1, HEAD_DIM), 1)
    j = (lane & (half - 1)).astype(F32)
    inv = jnp.exp(j * (-math.log(ROPE_BASE) / half))
    ang = pos * inv
    cos_ref[0] = jnp.cos(ang)
    sin_ref[0] = jnp.where(lane < half, -1.0, 1.0) * jnp.sin(ang)


def _rope_tables(positions, tr=1024):
    bsz, length = positions.shape
    tr = min(tr, length)
    spec_o = pl.BlockSpec((1, tr, HEAD_DIM), lambda bi, ti: (bi, ti, 0))
    return pl.pallas_call(
        _rope_kernel,
        grid=(bsz, length // tr),
        in_specs=[pl.BlockSpec((1, tr, 1), lambda bi, ti: (bi, ti, 0))],
        out_specs=[spec_o, spec_o],
        out_shape=[jax.ShapeDtypeStruct((bsz, length, HEAD_DIM), F32)] * 2,
        compiler_params=_params(("parallel", "parallel")),
        name="rope_tables",
    )(positions.reshape(bsz, length, 1))


def _ret_kernel(q_ref, k_ref, v_ref, g_ref, cos_ref, sin_ref, o_ref, s_scr):
    c = RET_CHUNK

    @pl.when(pl.program_id(1) == 0)
    def _():
        s_scr[...] = jnp.zeros_like(s_scr)

    cosf = cos_ref[0]
    sinf = sin_ref[0]
    rel = (lax.broadcasted_iota(jnp.int32, (c, c), 0) - lax.broadcasted_iota(jnp.int32, (c, c), 1)).astype(F32)
    idx = lax.broadcasted_iota(jnp.int32, (c, 1), 0).astype(F32)
    qa, ka, va, ga = q_ref[0], k_ref[0], v_ref[0], g_ref[0]
    outs = []
    for hh in range(HEADS):
        sl = slice(HEAD_DIM * hh, HEAD_DIM * (hh + 1))
        lg = math.log1p(-2.0 ** (-5.0 - hh))
        q = qa[:, sl]
        k = ka[:, sl]
        vb = va[:, sl].astype(BF16)
        qh = q * cosf + pltpu.roll(q, HEAD_DIM // 2, 1) * sinf
        kh = (k * cosf + pltpu.roll(k, HEAD_DIM // 2, 1) * sinf) * HEAD_DIM ** -0.5
        dmat = jnp.where(rel >= 0.0, jnp.exp(lg * jnp.maximum(rel, 0.0)), 0.0)
        scores = _dot_nt(qh.astype(BF16), kh.astype(BF16)) * dmat
        o = _dot(scores.astype(BF16), vb)
        q_dec = qh * jnp.exp(lg * (idx + 1.0))
        k_dec = kh * jnp.exp(lg * (c - 1.0 - idx))
        st = s_scr[hh]
        o = o + _dot(q_dec.astype(BF16), st.astype(BF16))
        s_scr[hh] = math.exp(lg * c) * st + _dot_tn(k_dec.astype(BF16), vb)
        mu = jnp.mean(o, axis=-1, keepdims=True)
        var = jnp.mean((o - mu) ** 2, axis=-1, keepdims=True)
        outs.append((o - mu) * lax.rsqrt(var + EPS))
    o_ref[0] = (jnp.concatenate(outs, axis=1) * _silu(ga)).astype(o_ref.dtype)


def _retention(proj3, cos_t, sin_t):
    bsz, length, _ = proj3.shape
    c = RET_CHUNK
    col = lambda k: pl.BlockSpec((1, c, MIX), lambda bi, ti, k=k: (bi, ti, k))
    tab = pl.BlockSpec((1, c, HEAD_DIM), lambda bi, ti: (bi, ti, 0))
    return pl.pallas_call(
        _ret_kernel,
        grid=(bsz, length // c),
        in_specs=[col(5), col(6), col(7), col(8), tab, tab],
        out_specs=pl.BlockSpec((1, c, MIX), lambda bi, ti: (bi, ti, 0)),
        out_shape=jax.ShapeDtypeStruct((bsz, length, MIX), BF16),
        scratch_shapes=[pltpu.VMEM((HEADS, HEAD_DIM, HEAD_DIM), F32)],
        compiler_params=_params(("parallel", "arbitrary")),
        name="retention",
    )(proj3, proj3, proj3, proj3, cos_t, sin_t)


def _memkv_kernel(m_ref, g_ref, w_ref, k_ref, v_ref):
    kv = _dot(_rms(m_ref[0], g_ref[...]).astype(BF16), w_ref[...])
    k_ref[0] = kv[:, :D_MODEL].astype(BF16)
    v_ref[0] = kv[:, D_MODEL:].astype(BF16)


def _memkv(mem, gain, wkv_bf16):
    bsz, m, d = mem.shape
    spec_o = pl.BlockSpec((1, m, d), lambda bi: (bi, 0, 0))
    return pl.pallas_call(
        _memkv_kernel,
        grid=(bsz,),
        in_specs=[pl.BlockSpec((1, m, d), lambda bi: (bi, 0, 0)),
                  pl.BlockSpec((1, d), lambda bi: (0, 0)),
                  pl.BlockSpec((d, 2 * d), lambda bi: (0, 0))],
        out_specs=[spec_o, spec_o],
        out_shape=[jax.ShapeDtypeStruct((bsz, m, d), BF16)] * 2,
        compiler_params=_params(("parallel",)),
        name="mem_kv",
    )(mem, gain, wkv_bf16)


def _mix_kernel(x_ref, ga_ref, gb_ref, bg_ref, hg_ref, s5_ref, rt_ref, wb_ref, wout_ref,
                nc_ref, k_ref, v_ref, wq_ref, wo_ref, o_ref):
    d = D_MODEL
    logits = jnp.concatenate([ga_ref[...], gb_ref[...]], axis=1) + bg_ref[...]
    gates = _sigmoid(logits)
    merged = (gates[:, 0:d] * _dot(hg_ref[...], wb_ref[0:MIX, :])
              + gates[:, d:2 * d] * _dot(s5_ref[...], wb_ref[MIX:MIX + S5_WIDTH, :])
              + gates[:, 2 * d:3 * d] * _dot(rt_ref[...], wb_ref[MIX + S5_WIDTH:, :]))
    x1 = x_ref[...] + _dot(merged.astype(BF16), wout_ref[...])

    hc = _rms(x1, nc_ref[...]).astype(BF16)
    q = _dot(hc, wq_ref[...])
    kk = k_ref[0]
    vv = v_ref[0]
    outs = []
    for hh in range(XA_HEADS):
        sl = slice(XA_DH * hh, XA_DH * (hh + 1))
        s = _dot_nt(q[:, sl].astype(BF16), kk[:, sl]) * XA_DH ** -0.5
        p = jnp.exp(s - jnp.max(s, axis=-1, keepdims=True))
        p = p / jnp.sum(p, axis=-1, keepdims=True)
        outs.append(_dot(p.astype(BF16), vv[:, sl]))
    attn = jnp.concatenate(outs, axis=1)
    o_ref[...] = x1 + _dot(attn.astype(BF16), wo_ref[...])


def _mix(x2d, proj, bgate, o_hg, o_s5, o_rt, wb, wout, ncross, kmem, vmem, wq, wo, length, tm=512):
    m, d = x2d.shape
    per_batch = length // tm
    tok = lambda w: pl.BlockSpec((tm, w), lambda i: (i, 0))
    const = lambda shape: pl.BlockSpec(shape, lambda i: tuple(0 for _ in shape))
    gw = 1536
    mem_spec = pl.BlockSpec((1,) + kmem.shape[1:], lambda i: (i // per_batch, 0, 0))
    return pl.pallas_call(
        _mix_kernel,
        grid=(m // tm,),
        in_specs=[tok(d),
                  pl.BlockSpec((tm, gw), lambda i: (i, 3)), pl.BlockSpec((tm, gw), lambda i: (i, 4)),
                  const((1, 3 * d)), tok(MIX), tok(S5_WIDTH), tok(MIX),
                  const(wb.shape), const((d, d)), const((1, d)), mem_spec, mem_spec,
                  const((d, d)), const((d, d))],
        out_specs=tok(d),
        out_shape=jax.ShapeDtypeStruct((m, d), F32),
        compiler_params=_params(("parallel",)),
        name="merge_xattn",
    )(x2d, proj, proj, bgate, o_hg, o_s5, o_rt, wb, wout, ncross, kmem, vmem, wq, wo)


def _top_extract(s, with_rank):
    t = s.shape[1]
    row16 = lax.broadcasted_iota(jnp.int32, (PEER_TOPK, t), 0)
    vals = jnp.zeros((PEER_TOPK, t), F32)
    rank = jnp.full(s.shape, UNRANKED, F32) if with_rank else None
    w = s
    for r in range(PEER_TOPK):
        m = jnp.max(w, axis=0, keepdims=True)
        hit = w == m
        if with_rank:
            rank = jnp.where(hit, float(r + 1), rank)
        w = jnp.where(hit, NEG_INF, w)
        vals = jnp.where(row16 == r, m, vals)
    return vals, rank


def _route_kernel(x_ref, g_ref, wqt_ref, keys_ref, h_ref, rank2_ref, e2_ref, n1_ref, e1_ref):
    t = x_ref.shape[0]
    k = PEER_TOPK
    hb = _rms(x_ref[...], g_ref[...]).astype(BF16)
    h_ref[...] = hb
    qt = _dot_nt(wqt_ref[...], hb).astype(BF16)
    row16 = lax.broadcasted_iota(jnp.int32, (k, t), 0)
    row8 = lax.broadcasted_iota(jnp.int32, (8, t), 0)
    zeros8 = jnp.zeros((8, t), F32)
    for hd in range(PEER_HEADS):
        base = hd * 2 * PEER_DH
        s1 = _dot(keys_ref[2 * hd], qt[base:base + PEER_DH])
        s2 = _dot(keys_ref[2 * hd + 1], qt[base + PEER_DH:base + 2 * PEER_DH])
        a, _ = _top_extract(s1, False)
        b, rank2 = _top_extract(s2, True)
        pieces = [a + b[0:1]]
        for j in range(2, 9):
            pieces.append(jnp.where(row8 < k // j, a[0:8] + b[j - 1:j], NEG_INF))
        pieces.append(a[0:1] + b[8:16])
        cand = jnp.concatenate(pieces, axis=0)
        top = a[0:1] + b[0:1]
        work = cand
        thr = top
        for r in range(k):
            thr = jnp.max(work, axis=0, keepdims=True)
            work = jnp.where(work == thr, NEG_INF, work)
        sel = cand >= thr
        zsum = jnp.sum(jnp.where(sel, jnp.exp(cand - top), 0.0), axis=0, keepdims=True)
        picked = sel.astype(F32)
        low = picked[16:24]
        for j in range(3, 9):
            low = low + picked[8 * j:8 * j + 8]
        cnt_tail = jnp.sum(picked[72:80], axis=0, keepdims=True)
        n_of_i = picked[0:16] + jnp.concatenate([low, zeros8], axis=0) + jnp.where(row16 == 0, cnt_tail, 0.0)
        n1 = jnp.zeros(s1.shape, F32)
        for i in range(k):
            n1 = jnp.where(s1 == a[i:i + 1], n_of_i[i:i + 1], n1)
        rows = slice(hd * PEER_NKEYS, (hd + 1) * PEER_NKEYS)
        rank2_ref[rows, :] = rank2.astype(BF16)
        e2_ref[rows, :] = (0.5 * jnp.exp(s2 - b[0:1]) / zsum).astype(BF16)
        n1_ref[rows, :] = n1
        e1_ref[rows, :] = jnp.exp(s1 - a[0:1])


def _route(x2d, gain, wqt, keys2, tt=256):
    m, d = x2d.shape
    rows = PEER_HEADS * PEER_NKEYS
    tab = pl.BlockSpec((rows, tt), lambda i: (0, i))
    return pl.pallas_call(
        _route_kernel,
        grid=(m // tt,),
        in_specs=[pl.BlockSpec((tt, d), lambda i: (i, 0)),
                  pl.BlockSpec((1, d), lambda i: (0, 0)),
                  pl.BlockSpec(wqt.shape, lambda i: (0, 0)),
                  pl.BlockSpec(keys2.shape, lambda i: (0, 0, 0))],
        out_specs=[pl.BlockSpec((tt, d), lambda i: (i, 0)), tab, tab, tab, tab],
        out_shape=[jax.ShapeDtypeStruct((m, d), BF16)] + [jax.ShapeDtypeStruct((rows, m), BF16)] * 2
        + [jax.ShapeDtypeStruct((rows, m), F32)] * 2,
        compiler_params=_params(("parallel",)),
        name="peer_route",
    )(x2d, gain, wqt, keys2)


def _peer_kernel(final, x_ref, h_ref, rank2_ref, e2_ref, n1_ref, e1_ref, u_ref, vt_ref, gf_ref,
                 o_ref, acc_scr):
    j = pl.program_id(1)
    eb = u_ref.shape[0]
    nk = PEER_NKEYS

    @pl.when(j == 0)
    def _():
        acc_scr[...] = jnp.zeros_like(acc_scr)

    tt = h_ref.shape[0]
    half = tt // 2
    i1_base = pl.multiple_of(j * (eb // nk), eb // nk)
    sizes = [s for s in PEER_CHUNK_BLOCKS]
    assert sum(sizes) * nk == eb
    starts = [sum(sizes[:c]) for c in range(len(sizes))]
    rows_of = [slice(st * nk, (st + sz) * nk) for st, sz in zip(starts, sizes)]

    def score(ck, th):
        return _dot_nt(u_ref[rows_of[ck], :], h_ref[th * half:(th + 1) * half, :])

    def key_row(ref, hd, il, cols):
        r = ref[pl.ds(hd * nk + i1_base + il, 1), :][:, cols]
        return jnp.tile(jnp.broadcast_to(r, (BF16_ROWS, half)).astype(BF16), (nk // BF16_ROWS, 1))

    def act_tile(z_half, ck, sub, th):
        il = starts[ck] + sub
        cols = slice(th * half, (th + 1) * half)
        gate = None
        for hd in range(PEER_HEADS):
            rows = slice(hd * nk, (hd + 1) * nk)
            sel = jnp.where(rank2_ref[rows, cols] <= key_row(n1_ref, hd, il, cols), e2_ref[rows, cols], 0.0)
            term = sel * key_row(e1_ref, hd, il, cols)
            gate = term if gate is None else gate + term
        return _gelu2(z_half[sub * nk:(sub + 1) * nk, :].astype(BF16)) * gate

    def apply(ck, th, act):
        cols = slice(th * half, (th + 1) * half)
        acc_scr[:, cols] += _dot(vt_ref[:, rows_of[ck]], act)

    z = [score(0, 0), score(0, 1)]
    prev = None
    for ck, n_sub in enumerate(sizes):
        pieces = []
        z_next = [None, None]
        if ck + 1 < len(sizes):
            pieces += [("score", 0), ("score", 1)]
        if prev is not None:
            pieces += [("apply", 0), ("apply", 1)]
        acts = ([], [])
        for sub in range(n_sub):
            for th in range(2):
                acts[th].append(act_tile(z[th], ck, sub, th))
            for kind, th in pieces[sub * len(pieces) // n_sub:(sub + 1) * len(pieces) // n_sub]:
                if kind == "score":
                    z_next[th] = score(ck + 1, th)
                else:
                    apply(ck - 1, th, prev[th])
        prev = [jnp.concatenate(acts[0], axis=0), jnp.concatenate(acts[1], axis=0)]
        z = z_next
    apply(len(sizes) - 1, 0, prev[0])
    apply(len(sizes) - 1, 1, prev[1])

    @pl.when(j == pl.num_programs(1) - 1)
    def _():
        y = x_ref[...] + acc_scr[...].T
        if final:
            y = _rms(y, gf_ref[...])
        o_ref[...] = y


def _peer(x2d, h, rank2, e2, n1, e1, u_bf16, vt_bf16, gfinal, final, tt=512, eb=2048):
    m, d = x2d.shape
    ne = u_bf16.shape[0]
    rows = PEER_HEADS * PEER_NKEYS
    tab = pl.BlockSpec((rows, tt), lambda i, j: (0, i))
    return pl.pallas_call(
        functools.partial(_peer_kernel, final),
        grid=(m // tt, ne // eb),
        in_specs=[pl.BlockSpec((tt, d), lambda i, j: (i, 0)),
                  pl.BlockSpec((tt, d), lambda i, j: (i, 0)),
                  tab, tab, tab, tab,
                  pl.BlockSpec((eb, d), lambda i, j: (j, 0)),
                  pl.BlockSpec((d, eb), lambda i, j: (0, j)),
                  pl.BlockSpec((1, d), lambda i, j: (0, 0))],
        out_specs=pl.BlockSpec((tt, d), lambda i, j: (i, 0)),
        out_shape=jax.ShapeDtypeStruct((m, d), F32),
        scratch_shapes=[pltpu.VMEM((d, tt), F32)],
        compiler_params=_params(("parallel", "arbitrary")),
        name="peer_experts",
    )(x2d, h, rank2, e2, n1, e1, u_bf16, vt_bf16, gfinal)


def kernel(x, mem, positions, norm_mix, w_in, b_gate, hgrn_lb, hgrn_norm, s5_lambda_re, s5_lambda_im,
           s5_log_step, s5_b_re, s5_b_im, s5_c_re, s5_c_im, s5_d, s5_w_glu, s5_b_glu, w_branch, w_out,
           norm_cross, norm_mem, xa_wq, xa_wkv, xa_wo, norm_ffn, peer_wq, peer_keys, peer_u, peer_v,
           norm_final):
    bsz, length, d = x.shape
    depth = w_in.shape[0]
    row = lambda a: a.astype(F32).reshape(1, -1)
    x2d = x.astype(F32).reshape(bsz * length, d)
    cos_t, sin_t = _rope_tables(positions)
    lb = hgrn_lb.astype(F32)
    for l in range(depth):
        proj = _inproj(x2d, row(norm_mix[l]), w_in[l].astype(BF16))
        proj3 = proj.reshape(bsz, length, IN_COLS)
        o_hg = _hgrn(proj3, lb, row(hgrn_norm[l]), l)
        bb_re, bb_im, tab = _s5_prep(s5_lambda_re[l], s5_lambda_im[l], s5_log_step[l], s5_b_re[l], s5_b_im[l])
        bbd = jnp.concatenate([_block_diag_in(bb_re), _block_diag_in(bb_im)], axis=2).astype(BF16)
        o_s5 = _s5(proj3, bbd, tab, _block_diag_out(s5_c_re[l]).astype(BF16),
                   _block_diag_out(s5_c_im[l]).astype(BF16), row(s5_d[l]),
                   s5_w_glu[l].astype(BF16), row(s5_b_glu[l]))
        o_rt = _retention(proj3, cos_t, sin_t)
        kmem, vmem = _memkv(mem.astype(F32), row(norm_mem[l]), xa_wkv[l].astype(BF16))
        x2d = _mix(x2d, proj, row(b_gate[l]), o_hg.reshape(-1, MIX), o_s5.reshape(-1, S5_WIDTH),
                   o_rt.reshape(-1, MIX), w_branch[l].astype(BF16), w_out[l].astype(BF16),
                   row(norm_cross[l]), kmem, vmem, xa_wq[l].astype(BF16), xa_wo[l].astype(BF16), length)
        keys2 = peer_keys[l].astype(BF16).reshape(PEER_HEADS * 2, PEER_NKEYS, PEER_DH)
        h, rank2, e2, n1, e1 = _route(x2d, row(norm_ffn[l]), peer_wq[l].astype(BF16).T, keys2)
        x2d = _peer(x2d, h, rank2, e2, n1, e1, peer_u[l].astype(BF16), peer_v[l].astype(BF16).T,
                    row(norm_final), final=(l == depth - 1))
    return x2d.reshape(bsz, length, d)
```

```python
import functools
import math

import jax
import jax.numpy as jnp
from jax import lax
from jax.experimental import pallas as pl
from jax.experimental.pallas import tpu as pltpu

F32 = jnp.float32
BF16 = jnp.bfloat16

D_MODEL = 1024
HEADS = 4
HEAD_DIM = 128
MIX = HEADS * HEAD_DIM
HG_F_MIN = 1e-6
S5_GROUPS = 32
S5_GROUP = 16
S5_STATE = 64
S5_WIDTH = S5_GROUPS * S5_GROUP
S5_LANES = S5_GROUPS * S5_STATE
S5_BLOCKS = 4
RET_CHUNK = 256
ROPE_BASE = 10000.0
IN_COLS = 4 * MIX + S5_WIDTH + 4 * MIX + 3 * D_MODEL
XA_HEADS = 4
XA_DH = D_MODEL // XA_HEADS
PEER_HEADS = 8
PEER_NKEYS = 128
PEER_DH = 128
PEER_TOPK = 16
EPS = 1e-6
NEG_INF = float("-inf")
UNRANKED = 99.0
BF16_ROWS = 16
PEER_CHUNK_BLOCKS = (4, 4, 4, 4)

VMEM_LIMIT = 56 * 1024 * 1024

NT_DIMS = (((1,), (1,)), ((), ()))
TN_DIMS = (((0,), (0,)), ((), ()))


def _params(sem):
    return pltpu.CompilerParams(dimension_semantics=sem, vmem_limit_bytes=VMEM_LIMIT)


def _rms(x, gain):
    return x * lax.rsqrt(jnp.mean(x * x, axis=-1, keepdims=True) + EPS) * gain


def _gelu(x):
    return 0.5 * x * (1.0 + jnp.tanh(math.sqrt(2.0 / math.pi) * (x + 0.044715 * (x * x * x))))


def _gelu2(x):
    c = math.sqrt(2.0 / math.pi)
    return x + x * jnp.tanh(x * (c + (c * 0.044715) * (x * x)))


def _sigmoid(x):
    return 1.0 / (1.0 + jnp.exp(-x))


def _silu(x):
    return x * _sigmoid(x)


def _dot(a, b):
    return jnp.dot(a, b, preferred_element_type=F32)


def _dot_nt(a, b):
    return lax.dot_general(a, b, NT_DIMS, preferred_element_type=F32)


def _dot_tn(a, b):
    return lax.dot_general(a, b, TN_DIMS, preferred_element_type=F32)


def _inproj_kernel(x_ref, g_ref, w_ref, o_ref, h_scr):
    @pl.when(pl.program_id(1) == 0)
    def _():
        h_scr[...] = _rms(x_ref[...], g_ref[...]).astype(BF16)

    o_ref[...] = _dot(h_scr[...], w_ref[...])


def _inproj(x2d, gain, w_bf16, tm=1024, tn=2560):
    m, d = x2d.shape
    n = w_bf16.shape[1]
    return pl.pallas_call(
        _inproj_kernel,
        grid=(m // tm, n // tn),
        in_specs=[
            pl.BlockSpec((tm, d), lambda i, j: (i, 0)),
            pl.BlockSpec((1, d), lambda i, j: (0, 0)),
            pl.BlockSpec((d, tn), lambda i, j: (0, j)),
        ],
        out_specs=pl.BlockSpec((tm, tn), lambda i, j: (i, j)),
        out_shape=jax.ShapeDtypeStruct((m, n), F32),
        scratch_shapes=[pltpu.VMEM((tm, d), BF16)],
        compiler_params=_params(("parallel", "arbitrary")),
        name="inproj",
    )(x2d, gain, w_bf16)


def _hgrn_kernel(layer, q_ref, f_ref, i_ref, g_ref, lb_ref, ng_ref, o_ref, st_scr, lvl_scr):
    tc = q_ref.shape[1]
    n_lvl = tc.bit_length() - 1

    @pl.when(pl.program_id(1) == 0)
    def _():
        st_scr[...] = jnp.zeros_like(st_scr)
        ti = lax.broadcasted_iota(jnp.int32, (tc, tc), 0)
        si = lax.broadcasted_iota(jnp.int32, (tc, tc), 1)
        top_bit = (lax.bitcast_convert_type((ti ^ si).astype(F32), jnp.int32) >> 23) - 127
        lvl_scr[...] = jnp.where(si < ti, top_bit, jnp.where(si == ti, n_lvl, -1))

    lbs = lb_ref[...]
    e = jnp.exp(lbs - jnp.max(lbs, axis=0, keepdims=True))
    sm = e / jnp.sum(e, axis=0, keepdims=True)
    lower = jnp.sum(sm[0:layer + 1], axis=0, keepdims=True) - sm[0:1]

    z = f_ref[0]
    fv = lower + (1.0 - lower) * _sigmoid(z)
    logf = jnp.log(jnp.clip(fv, HG_F_MIN, 1.0))
    kk = (1.0 - lower) * _sigmoid(-z)
    qs = q_ref[0] * HEAD_DIM ** -0.5
    vb = i_ref[0].astype(BF16)
    heads = [slice(HEAD_DIM * hh, HEAD_DIM * (hh + 1)) for hh in range(HEADS)]

    lvl = lvl_scr[...]
    qb = qs.astype(BF16)
    kb = kk.astype(BF16)
    att = [jnp.where(lvl == n_lvl, _dot_nt(qb[:, sl], kb[:, sl]), 0.0) for sl in heads]

    rowi = lax.broadcasted_iota(jnp.int32, (tc, MIX), 0)
    pre = logf
    post = jnp.zeros_like(logf)
    for level in range(n_lvl):
        h = 1 << level
        upper = (rowi & h) != 0
        x = (jnp.where(upper, qs, kk) * jnp.exp(jnp.where(upper, pre, post))).astype(BF16)
        here = lvl == level
        att = [jnp.where(here, _dot_nt(x[:, sl], x[:, sl]), a) for sl, a in zip(heads, att)]
        total = pre + post
        pre = pre + jnp.where(upper, pltpu.roll(total, h, 0), 0.0)
        post = post + jnp.where(upper, 0.0, pltpu.roll(total, tc - h, 0))

    qd = (qs * jnp.exp(pre)).astype(BF16)
    kd = (kk * jnp.exp(post)).astype(BF16)
    decay = jnp.exp(pre[tc - 1:tc, :])
    gate = g_ref[0]
    ng = ng_ref[...]
    outs = []
    for hh, sl in enumerate(heads):
        st = st_scr[hh]
        oh = _dot(att[hh].astype(BF16), vb[:, sl]) + _dot_nt(qd[:, sl], st.astype(BF16))
        st_scr[hh] = st * decay[:, sl] + _dot_tn(vb[:, sl], kd[:, sl])
        outs.append(oh * lax.rsqrt(jnp.mean(oh * oh, axis=-1, keepdims=True) + EPS) * ng[:, sl])
    o_ref[0] = (jnp.concatenate(outs, axis=1) * _silu(gate)).astype(o_ref.dtype)


def _hgrn(proj3, lb, ng, layer, tc=256):
    bsz, length, _ = proj3.shape
    tc = min(tc, length)
    col = lambda k: pl.BlockSpec((1, tc, MIX), lambda bi, ti, k=k: (bi, ti, k))
    return pl.pallas_call(
        functools.partial(_hgrn_kernel, layer),
        grid=(bsz, length // tc),
        in_specs=[col(0), col(1), col(2), col(3),
                  pl.BlockSpec(lb.shape, lambda bi, ti: (0, 0)),
                  pl.BlockSpec((1, MIX), lambda bi, ti: (0, 0))],
        out_specs=pl.BlockSpec((1, tc, MIX), lambda bi, ti: (bi, ti, 0)),
        out_shape=jax.ShapeDtypeStruct((bsz, length, MIX), BF16),
        scratch_shapes=[pltpu.VMEM((HEADS, HEAD_DIM, HEAD_DIM), F32), pltpu.VMEM((tc, tc), jnp.int32)],
        compiler_params=_params(("parallel", "arbitrary")),
        name="hgrn2",
    )(proj3, proj3, proj3, proj3, lb, ng)


def _s5_prep_kernel(lr_ref, li_ref, ls_ref, bre_ref, bim_ref, bbre_ref, bbim_ref, tab_ref):
    lr = jnp.minimum(lr_ref[...], -1e-4)
    li = li_ref[...]
    dt = jnp.exp(ls_ref[...])
    mag = jnp.exp(lr * dt)
    a_re = mag * jnp.cos(li * dt)
    a_im = mag * jnp.sin(li * dt)
    den = lr * lr + li * li
    z_re = ((a_re - 1.0) * lr + a_im * li) / den
    z_im = (a_im * lr - (a_re - 1.0) * li) / den
    bre = bre_ref[...]
    bim = bim_ref[...]
    bbre_ref[...] = z_re * bre - z_im * bim
    bbim_ref[...] = z_re * bim + z_im * bre
    row = lax.broadcasted_iota(jnp.int32, (8, a_re.shape[1]), 0)
    p_re = jnp.broadcast_to(a_re, row.shape)
    p_im = jnp.broadcast_to(a_im, row.shape)
    c_re, c_im = a_re, a_im
    powers = [(a_re, a_im)]
    for r in range(1, 8):
        c_re, c_im = c_re * a_re - c_im * a_im, c_re * a_im + c_im * a_re
        powers.append((c_re, c_im))
        p_re = jnp.where(row == r, c_re, p_re)
        p_im = jnp.where(row == r, c_im, p_im)
    for slot, d in enumerate((1, 2, 4)):
        tab_ref[2 * slot] = jnp.where(row >= d, powers[d - 1][0], 0.0)
        tab_ref[2 * slot + 1] = jnp.where(row >= d, powers[d - 1][1], 0.0)
    tab_ref[6] = p_re
    tab_ref[7] = p_im


def _s5_prep(lam_re, lam_im, log_step, b_re, b_im):
    lanes = S5_LANES
    row = lambda a: a.astype(F32).reshape(1, lanes)
    ls = jnp.repeat(log_step.astype(F32), S5_STATE).reshape(1, lanes)
    bt = lambda a: a.astype(F32).reshape(lanes, S5_GROUP).T
    full = lambda shape: pl.BlockSpec(shape, lambda: tuple(0 for _ in shape))
    return pl.pallas_call(
        _s5_prep_kernel,
        in_specs=[full((1, lanes))] * 3 + [full((S5_GROUP, lanes))] * 2,
        out_specs=[full((S5_GROUP, lanes))] * 2 + [full((8, 8, lanes))],
        out_shape=[jax.ShapeDtypeStruct((S5_GROUP, lanes), F32)] * 2
        + [jax.ShapeDtypeStruct((8, 8, lanes), F32)],
        name="s5_prep",
    )(row(lam_re), row(lam_im), ls, bt(b_re), bt(b_im))


def _s5_kernel(u_ref, bbd_ref, tab_ref, cre_ref, cim_ref, d_ref, wg_ref, bg_ref, o_ref,
               xr_scr, xi_scr, car_scr):
    tt = u_ref.shape[1]
    lanes = S5_LANES
    lw = 512

    @pl.when(pl.program_id(1) == 0)
    def _():
        car_scr[...] = jnp.zeros_like(car_scr)

    u = u_ref[0]
    ub = u.astype(BF16)
    n_blk = bbd_ref.shape[0]
    cw = S5_WIDTH // n_blk
    bl = lanes // n_blk
    for jb in range(n_blk):
        bu = _dot(ub[:, cw * jb:cw * (jb + 1)], bbd_ref[jb])
        xr_scr[:, bl * jb:bl * (jb + 1)] = bu[:, :bl]
        xi_scr[:, bl * jb:bl * (jb + 1)] = bu[:, bl:]

    for lc in range(lanes // lw):
        sl = slice(lw * lc, lw * (lc + 1))
        steps = [(d, tab_ref[2 * s][:, sl], tab_ref[2 * s + 1][:, sl]) for s, d in enumerate((1, 2, 4))]
        p_re = tab_ref[6][:, sl]
        p_im = tab_ref[7][:, sl]

        def body(k, carry, sl=sl, steps=steps, p_re=p_re, p_im=p_im):
            c_re, c_im = carry
            r0 = pl.multiple_of(k * 8, 8)
            xr = xr_scr[pl.ds(r0, 8), sl]
            xi = xi_scr[pl.ds(r0, 8), sl]
            for d, a_re, a_im in steps:
                sr = pltpu.roll(xr, d, 0)
                si = pltpu.roll(xi, d, 0)
                xr, xi = xr + a_re * sr - a_im * si, xi + a_re * si + a_im * sr
            xr, xi = xr + p_re * c_re - p_im * c_im, xi + p_re * c_im + p_im * c_re
            xr_scr[pl.ds(r0, 8), sl] = xr
            xi_scr[pl.ds(r0, 8), sl] = xi
            return xr[7:8, :], xi[7:8, :]

        c_re, c_im = lax.fori_loop(0, tt // 8, body, (car_scr[0:1, sl], car_scr[1:2, sl]), unroll=2)
        car_scr[0:1, sl] = c_re
        car_scr[1:2, sl] = c_im

    ys = []
    for jb in range(n_blk):
        cols = slice(bl * jb, bl * (jb + 1))
        ys.append(_dot(xr_scr[:, cols].astype(BF16), cre_ref[jb]) - _dot(xi_scr[:, cols].astype(BF16), cim_ref[jb]))
    y = jnp.concatenate(ys, axis=1) + d_ref[...] * u
    y = _gelu(y)
    o_ref[0] = (y * _sigmoid(_dot(y.astype(BF16), wg_ref[...]) + bg_ref[...])).astype(o_ref.dtype)


def _s5(proj3, bbd, tab, cre, cim, dskip, wg, bg, tt=256):
    bsz, length, _ = proj3.shape
    lanes = S5_LANES
    const = lambda shape: pl.BlockSpec(shape, lambda bi, ti: tuple(0 for _ in shape))
    return pl.pallas_call(
        _s5_kernel,
        grid=(bsz, length // tt),
        in_specs=[pl.BlockSpec((1, tt, S5_WIDTH), lambda bi, ti: (bi, ti, 4)),
                  const(bbd.shape), const((8, 8, lanes)), const(cre.shape), const(cim.shape),
                  const((1, S5_WIDTH)), const((S5_WIDTH, S5_WIDTH)), const((1, S5_WIDTH))],
        out_specs=pl.BlockSpec((1, tt, S5_WIDTH), lambda bi, ti: (bi, ti, 0)),
        out_shape=jax.ShapeDtypeStruct((bsz, length, S5_WIDTH), BF16),
        scratch_shapes=[pltpu.VMEM((tt, lanes), F32), pltpu.VMEM((tt, lanes), F32),
                        pltpu.VMEM((8, lanes), F32)],
        compiler_params=_params(("parallel", "arbitrary")),
        name="s5",
    )(proj3, bbd, tab, cre, cim, dskip, wg, bg)


def _block_diag_in(bb_t):
    g, c, p = S5_GROUPS, S5_GROUP, S5_STATE
    gb = g // S5_BLOCKS
    per_group = bb_t.reshape(c, S5_BLOCKS, gb, p).transpose(1, 2, 0, 3)
    eye = jnp.eye(gb, dtype=F32)
    full = eye[None, :, None, :, None] * per_group[:, :, :, None, :]
    return full.reshape(S5_BLOCKS, gb * c, gb * p)


def _block_diag_out(cmat):
    g, c, p = S5_GROUPS, S5_GROUP, S5_STATE
    gb = g // S5_BLOCKS
    per_group = cmat.astype(F32).reshape(S5_BLOCKS, gb, c, p).transpose(0, 1, 3, 2)
    eye = jnp.eye(gb, dtype=F32)
    full = eye[None, :, None, :, None] * per_group[:, :, :, None, :]
    return full.reshape(S5_BLOCKS, gb * p, gb * c)


def _rope_kernel(pos_ref, cos_ref, sin_ref):
    pos = pos_ref[0].astype(F32)
    half = HEAD_DIM // 2
    lane = lax.broadcasted_iota(jnp.int32, (1, HEAD_DIM), 1)
    j = (lane & (half - 1)).astype(F32)
    inv = jnp.exp(j * (-math.log(ROPE_BASE) / half))
    ang = pos * inv
    cos_ref[0] = jnp.cos(ang)
    sin_ref[0] = jnp.where(lane < half, -1.0, 1.0) * jnp.sin(ang)


def _rope_tables(positions, tr=1024):
    bsz, length = positions.shape
    tr = min(tr, length)
    spec_o = pl.BlockSpec((1, tr, HEAD_DIM), lambda bi, ti: (bi, ti, 0))
    return pl.pallas_call(
        _rope_kernel,
        grid=(bsz, length // tr),
        in_specs=[pl.BlockSpec((1, tr, 1), lambda bi, ti: (bi, ti, 0))],
        out_specs=[spec_o, spec_o],
        out_shape=[jax.ShapeDtypeStruct((bsz, length, HEAD_DIM), F32)] * 2,
        compiler_params=_params(("parallel", "parallel")),
        name="rope_tables",
    )(positions.reshape(bsz, length, 1))


def _ret_kernel(q_ref, k_ref, v_ref, g_ref, cos_ref, sin_ref, o_ref, s_scr):
    c = RET_CHUNK

    @pl.when(pl.program_id(1) == 0)
    def _():
        s_scr[...] = jnp.zeros_like(s_scr)

    cosf = cos_ref[0]
    sinf = sin_ref[0]
    rel = (lax.broadcasted_iota(jnp.int32, (c, c), 0) - lax.broadcasted_iota(jnp.int32, (c, c), 1)).astype(F32)
    idx = lax.broadcasted_iota(jnp.int32, (c, 1), 0).astype(F32)
    qa, ka, va, ga = q_ref[0], k_ref[0], v_ref[0], g_ref[0]
    outs = []
    for hh in range(HEADS):
        sl = slice(HEAD_DIM * hh, HEAD_DIM * (hh + 1))
        lg = math.log1p(-2.0 ** (-5.0 - hh))
        q = qa[:, sl]
        k = ka[:, sl]
        vb = va[:, sl].astype(BF16)
        qh = q * cosf + pltpu.roll(q, HEAD_DIM // 2, 1) * sinf
        kh = (k * cosf + pltpu.roll(k, HEAD_DIM // 2, 1) * sinf) * HEAD_DIM ** -0.5
        dmat = jnp.where(rel >= 0.0, jnp.exp(lg * jnp.maximum(rel, 0.0)), 0.0)
        scores = _dot_nt(qh.astype(BF16), kh.astype(BF16)) * dmat
        o = _dot(scores.astype(BF16), vb)
        q_dec = qh * jnp.exp(lg * (idx + 1.0))
        k_dec = kh * jnp.exp(lg * (c - 1.0 - idx))
        st = s_scr[hh]
        o = o + _dot(q_dec.astype(BF16), st.astype(BF16))
        s_scr[hh] = math.exp(lg * c) * st + _dot_tn(k_dec.astype(BF16), vb)
        mu = jnp.mean(o, axis=-1, keepdims=True)
        var = jnp.mean((o - mu) ** 2, axis=-1, keepdims=True)
        outs.append((o - mu) * lax.rsqrt(var + EPS))
    o_ref[0] = (jnp.concatenate(outs, axis=1) * _silu(ga)).astype(o_ref.dtype)


def _retention(proj3, cos_t, sin_t):
    bsz, length, _ = proj3.shape
    c = RET_CHUNK
    col = lambda k: pl.BlockSpec((1, c, MIX), lambda bi, ti, k=k: (bi, ti, k))
    tab = pl.BlockSpec((1, c, HEAD_DIM), lambda bi, ti: (bi, ti, 0))
    return pl.pallas_call(
        _ret_kernel,
        grid=(bsz, length // c),
        in_specs=[col(5), col(6), col(7), col(8), tab, tab],
        out_specs=pl.BlockSpec((1, c, MIX), lambda bi, ti: (bi, ti, 0)),
        out_shape=jax.ShapeDtypeStruct((bsz, length, MIX), BF16),
        scratch_shapes=[pltpu.VMEM((HEADS, HEAD_DIM, HEAD_DIM), F32)],
        compiler_params=_params(("parallel", "arbitrary")),
        name="retention",
    )(proj3, proj3, proj3, proj3, cos_t, sin_t)


def _memkv_kernel(m_ref, g_ref, w_ref, k_ref, v_ref):
    kv = _dot(_rms(m_ref[0], g_ref[...]).astype(BF16), w_ref[...])
    k_ref[0] = kv[:, :D_MODEL].astype(BF16)
    v_ref[0] = kv[:, D_MODEL:].astype(BF16)


def _memkv(mem, gain, wkv_bf16):
    bsz, m, d = mem.shape
    spec_o = pl.BlockSpec((1, m, d), lambda bi: (bi, 0, 0))
    return pl.pallas_call(
        _memkv_kernel,
        grid=(bsz,),
        in_specs=[pl.BlockSpec((1, m, d), lambda bi: (bi, 0, 0)),
                  pl.BlockSpec((1, d), lambda bi: (0, 0)),
                  pl.BlockSpec((d, 2 * d), lambda bi: (0, 0))],
        out_specs=[spec_o, spec_o],
        out_shape=[jax.ShapeDtypeStruct((bsz, m, d), BF16)] * 2,
        compiler_params=_params(("parallel",)),
        name="mem_kv",
    )(mem, gain, wkv_bf16)


def _mix_kernel(x_ref, ga_ref, gb_ref, bg_ref, hg_ref, s5_ref, rt_ref, wb_ref, wout_ref,
                nc_ref, k_ref, v_ref, wq_ref, wo_ref, o_ref):
    d = D_MODEL
    logits = jnp.concatenate([ga_ref[...], gb_ref[...]], axis=1) + bg_ref[...]
    gates = _sigmoid(logits)
    merged = (gates[:, 0:d] * _dot(hg_ref[...], wb_ref[0:MIX, :])
              + gates[:, d:2 * d] * _dot(s5_ref[...], wb_ref[MIX:MIX + S5_WIDTH, :])
              + gates[:, 2 * d:3 * d] * _dot(rt_ref[...], wb_ref[MIX + S5_WIDTH:, :]))
    x1 = x_ref[...] + _dot(merged.astype(BF16), wout_ref[...])

    hc = _rms(x1, nc_ref[...]).astype(BF16)
    q = _dot(hc, wq_ref[...])
    kk = k_ref[0]
    vv = v_ref[0]
    outs = []
    for hh in range(XA_HEADS):
        sl = slice(XA_DH * hh, XA_DH * (hh + 1))
        s = _dot_nt(q[:, sl].astype(BF16), kk[:, sl]) * XA_DH ** -0.5
        p = jnp.exp(s - jnp.max(s, axis=-1, keepdims=True))
        p = p / jnp.sum(p, axis=-1, keepdims=True)
        outs.append(_dot(p.astype(BF16), vv[:, sl]))
    attn = jnp.concatenate(outs, axis=1)
    o_ref[...] = x1 + _dot(attn.astype(BF16), wo_ref[...])


def _mix(x2d, proj, bgate, o_hg, o_s5, o_rt, wb, wout, ncross, kmem, vmem, wq, wo, length, tm=512):
    m, d = x2d.shape
    per_batch = length // tm
    tok = lambda w: pl.BlockSpec((tm, w), lambda i: (i, 0))
    const = lambda shape: pl.BlockSpec(shape, lambda i: tuple(0 for _ in shape))
    gw = 1536
    mem_spec = pl.BlockSpec((1,) + kmem.shape[1:], lambda i: (i // per_batch, 0, 0))
    return pl.pallas_call(
        _mix_kernel,
        grid=(m // tm,),
        in_specs=[tok(d),
                  pl.BlockSpec((tm, gw), lambda i: (i, 3)), pl.BlockSpec((tm, gw), lambda i: (i, 4)),
                  const((1, 3 * d)), tok(MIX), tok(S5_WIDTH), tok(MIX),
                  const(wb.shape), const((d, d)), const((1, d)), mem_spec, mem_spec,
                  const((d, d)), const((d, d))],
        out_specs=tok(d),
        out_shape=jax.ShapeDtypeStruct((m, d), F32),
        compiler_params=_params(("parallel",)),
        name="merge_xattn",
    )(x2d, proj, proj, bgate, o_hg, o_s5, o_rt, wb, wout, ncross, kmem, vmem, wq, wo)


def _sorting_network(n):
    pairs = []
    p = 1
    while p < n:
        k = p
        while k >= 1:
            for j in range(k % p, n - k, 2 * k):
                for i in range(min(k, n - j - k)):
                    if (i + j) // (2 * p) == (i + j + k) // (2 * p):
                        pairs.append((i + j, i + j + k))
            k //= 2
        p *= 2
    return pairs


def _top_values(s):
    n_keys, t = s.shape
    slabs = [s[8 * i:8 * (i + 1), :] for i in range(n_keys // 8)]
    for lo, hi in _sorting_network(len(slabs)):
        slabs[lo], slabs[hi] = jnp.maximum(slabs[lo], slabs[hi]), jnp.minimum(slabs[lo], slabs[hi])
    row16 = lax.broadcasted_iota(jnp.int32, (PEER_TOPK, t), 0)
    vals = jnp.zeros((PEER_TOPK, t), F32)
    for r in range(PEER_TOPK):
        m = jnp.max(slabs[0], axis=0, keepdims=True)
        hit = slabs[0] == m
        for i in range(PEER_TOPK - 1 - r):
            slabs[i] = jnp.where(hit, slabs[i + 1], slabs[i])
        vals = jnp.where(row16 == r, m, vals)
    return vals


def _route_kernel(x_ref, g_ref, wqt_ref, keys_ref, h_ref, rank2_ref, e2_ref, n1_ref, e1_ref):
    t = x_ref.shape[0]
    k = PEER_TOPK
    hb = _rms(x_ref[...], g_ref[...]).T.astype(BF16)
    h_ref[...] = hb
    qt = _dot(wqt_ref[...], hb).astype(BF16)
    row16 = lax.broadcasted_iota(jnp.int32, (k, t), 0)
    row8 = lax.broadcasted_iota(jnp.int32, (8, t), 0)
    zeros8 = jnp.zeros((8, t), F32)
    for hd in range(PEER_HEADS):
        base = hd * 2 * PEER_DH
        s1 = _dot(keys_ref[2 * hd], qt[base:base + PEER_DH])
        s2 = _dot(keys_ref[2 * hd + 1], qt[base + PEER_DH:base + 2 * PEER_DH])
        a = _top_values(s1)
        b = _top_values(s2)
        rank2 = jnp.full(s2.shape, UNRANKED, F32)
        for jr in range(k, 0, -1):
            rank2 = jnp.where(s2 >= b[jr - 1:jr], float(jr), rank2)
        pieces = [a + b[0:1]]
        for j in range(2, 9):
            pieces.append(jnp.where(row8 < k // j, a[0:8] + b[j - 1:j], NEG_INF))
        pieces.append(a[0:1] + b[8:16])
        cand = jnp.concatenate(pieces, axis=0)
        top = a[0:1] + b[0:1]
        work = cand
        thr = top
        for r in range(k):
            thr = jnp.max(work, axis=0, keepdims=True)
            work = jnp.where(work == thr, NEG_INF, work)
        sel = cand >= thr
        zsum = jnp.sum(jnp.where(sel, jnp.exp(cand - top), 0.0), axis=0, keepdims=True)
        picked = sel.astype(F32)
        low = picked[16:24]
        for j in range(3, 9):
            low = low + picked[8 * j:8 * j + 8]
        cnt_tail = jnp.sum(picked[72:80], axis=0, keepdims=True)
        n_of_i = picked[0:16] + jnp.concatenate([low, zeros8], axis=0) + jnp.where(row16 == 0, cnt_tail, 0.0)
        n1 = jnp.zeros(s1.shape, F32)
        for i in range(k):
            n1 = jnp.where(s1 == a[i:i + 1], n_of_i[i:i + 1], n1)
        rows = slice(hd * PEER_NKEYS, (hd + 1) * PEER_NKEYS)
        rank2_ref[rows, :] = rank2.astype(BF16)
        e2_ref[rows, :] = (0.5 * jnp.exp(s2 - b[0:1]) / zsum).astype(BF16)
        n1_ref[rows, :] = n1
        e1_ref[rows, :] = jnp.exp(s1 - a[0:1])


def _route(x2d, gain, wqt, keys2, tt=256):
    m, d = x2d.shape
    rows = PEER_HEADS * PEER_NKEYS
    tab = pl.BlockSpec((rows, tt), lambda i: (0, i))
    return pl.pallas_call(
        _route_kernel,
        grid=(m // tt,),
        in_specs=[pl.BlockSpec((tt, d), lambda i: (i, 0)),
                  pl.BlockSpec((1, d), lambda i: (0, 0)),
                  pl.BlockSpec(wqt.shape, lambda i: (0, 0)),
                  pl.BlockSpec(keys2.shape, lambda i: (0, 0, 0))],
        out_specs=[pl.BlockSpec((d, tt), lambda i: (0, i)), tab, tab, tab, tab],
        out_shape=[jax.ShapeDtypeStruct((d, m), BF16)] + [jax.ShapeDtypeStruct((rows, m), BF16)] * 2
        + [jax.ShapeDtypeStruct((rows, m), F32)] * 2,
        compiler_params=_params(("parallel",)),
        name="peer_route",
    )(x2d, gain, wqt, keys2)


def _peer_kernel(final, x_ref, h_ref, rank2_ref, e2_ref, n1_ref, e1_ref, u_ref, vt_ref, gf_ref,
                 o_ref, acc_scr):
    j = pl.program_id(1)
    eb = u_ref.shape[0]
    nk = PEER_NKEYS

    @pl.when(j == 0)
    def _():
        acc_scr[...] = jnp.zeros_like(acc_scr)

    tt = h_ref.shape[1]
    half = tt // 2
    i1_base = pl.multiple_of(j * (eb // nk), eb // nk)
    sizes = [s for s in PEER_CHUNK_BLOCKS]
    assert sum(sizes) * nk == eb
    starts = [sum(sizes[:c]) for c in range(len(sizes))]
    rows_of = [slice(st * nk, (st + sz) * nk) for st, sz in zip(starts, sizes)]

    def score(ck, th):
        return jnp.dot(u_ref[rows_of[ck], :], h_ref[:, th * half:(th + 1) * half],
                       preferred_element_type=F32).astype(BF16)

    def key_row(ref, hd, il, cols):
        r = ref[pl.ds(hd * nk + i1_base + il, 1), :][:, cols]
        return jnp.tile(jnp.broadcast_to(r, (BF16_ROWS, half)).astype(BF16), (nk // BF16_ROWS, 1))

    def act_tile(z_half, ck, sub, th):
        il = starts[ck] + sub
        cols = slice(th * half, (th + 1) * half)
        gate = None
        for hd in range(PEER_HEADS):
            rows = slice(hd * nk, (hd + 1) * nk)
            sel = jnp.where(rank2_ref[rows, cols] <= key_row(n1_ref, hd, il, cols), e2_ref[rows, cols], 0.0)
            term = sel * key_row(e1_ref, hd, il, cols)
            gate = term if gate is None else gate + term
        return _gelu2(z_half[sub * nk:(sub + 1) * nk, :]) * gate

    def apply(ck, th, act):
        cols = slice(th * half, (th + 1) * half)
        acc_scr[:, cols] += _dot(vt_ref[:, rows_of[ck]], act)

    z = [score(0, 0), score(0, 1)]
    prev = None
    for ck, n_sub in enumerate(sizes):
        pieces = []
        z_next = [None, None]
        if ck + 1 < len(sizes):
            pieces += [("score", 0), ("score", 1)]
        if prev is not None:
            pieces += [("apply", 0), ("apply", 1)]
        acts = ([], [])
        for sub in range(n_sub):
            for th in range(2):
                acts[th].append(act_tile(z[th], ck, sub, th))
            for kind, th in pieces[sub * len(pieces) // n_sub:(sub + 1) * len(pieces) // n_sub]:
                if kind == "score":
                    z_next[th] = score(ck + 1, th)
                else:
                    apply(ck - 1, th, prev[th])
        prev = [jnp.concatenate(acts[0], axis=0), jnp.concatenate(acts[1], axis=0)]
        z = z_next
    apply(len(sizes) - 1, 0, prev[0])
    apply(len(sizes) - 1, 1, prev[1])

    @pl.when(j == pl.num_programs(1) - 1)
    def _():
        y = x_ref[...] + acc_scr[...].T
        if final:
            y = _rms(y, gf_ref[...])
        o_ref[...] = y


def _peer(x2d, h, rank2, e2, n1, e1, u_bf16, vt_bf16, gfinal, final, tt=512, eb=2048):
    m, d = x2d.shape
    ne = u_bf16.shape[0]
    rows = PEER_HEADS * PEER_NKEYS
    tab = pl.BlockSpec((rows, tt), lambda i, j: (0, i))
    return pl.pallas_call(
        functools.partial(_peer_kernel, final),
        grid=(m // tt, ne // eb),
        in_specs=[pl.BlockSpec((tt, d), lambda i, j: (i, 0)),
                  pl.BlockSpec((d, tt), lambda i, j: (0, i)),
                  tab, tab, tab, tab,
                  pl.BlockSpec((eb, d), lambda i, j: (j, 0)),
                  pl.BlockSpec((d, eb), lambda i, j: (0, j)),
                  pl.BlockSpec((1, d), lambda i, j: (0, 0))],
        out_specs=pl.BlockSpec((tt, d), lambda i, j: (i, 0)),
        out_shape=jax.ShapeDtypeStruct((m, d), F32),
        scratch_shapes=[pltpu.VMEM((d, tt), F32)],
        compiler_params=_params(("parallel", "arbitrary")),
        name="peer_experts",
    )(x2d, h, rank2, e2, n1, e1, u_bf16, vt_bf16, gfinal)


def kernel(x, mem, positions, norm_mix, w_in, b_gate, hgrn_lb, hgrn_norm, s5_lambda_re, s5_lambda_im,
           s5_log_step, s5_b_re, s5_b_im, s5_c_re, s5_c_im, s5_d, s5_w_glu, s5_b_glu, w_branch, w_out,
           norm_cross, norm_mem, xa_wq, xa_wkv, xa_wo, norm_ffn, peer_wq, peer_keys, peer_u, peer_v,
           norm_final):
    bsz, length, d = x.shape
    depth = w_in.shape[0]
    row = lambda a: a.astype(F32).reshape(1, -1)
    x2d = x.astype(F32).reshape(bsz * length, d)
    cos_t, sin_t = _rope_tables(positions)
    lb = hgrn_lb.astype(F32)
    for l in range(depth):
        proj = _inproj(x2d, row(norm_mix[l]), w_in[l].astype(BF16))
        proj3 = proj.reshape(bsz, length, IN_COLS)
        o_hg = _hgrn(proj3, lb, row(hgrn_norm[l]), l)
        bb_re, bb_im, tab = _s5_prep(s5_lambda_re[l], s5_lambda_im[l], s5_log_step[l], s5_b_re[l], s5_b_im[l])
        bbd = jnp.concatenate([_block_diag_in(bb_re), _block_diag_in(bb_im)], axis=2).astype(BF16)
        o_s5 = _s5(proj3, bbd, tab, _block_diag_out(s5_c_re[l]).astype(BF16),
                   _block_diag_out(s5_c_im[l]).astype(BF16), row(s5_d[l]),
                   s5_w_glu[l].astype(BF16), row(s5_b_glu[l]))
        o_rt = _retention(proj3, cos_t, sin_t)
        kmem, vmem = _memkv(mem.astype(F32), row(norm_mem[l]), xa_wkv[l].astype(BF16))
        x2d = _mix(x2d, proj, row(b_gate[l]), o_hg.reshape(-1, MIX), o_s5.reshape(-1, S5_WIDTH),
                   o_rt.reshape(-1, MIX), w_branch[l].astype(BF16), w_out[l].astype(BF16),
                   row(norm_cross[l]), kmem, vmem, xa_wq[l].astype(BF16), xa_wo[l].astype(BF16), length)
        keys2 = peer_keys[l].astype(BF16).reshape(PEER_HEADS * 2, PEER_NKEYS, PEER_DH)
        h, rank2, e2, n1, e1 = _route(x2d, row(norm_ffn[l]), peer_wq[l].astype(BF16).T, keys2)
        x2d = _peer(x2d, h, rank2, e2, n1, e1, peer_u[l].astype(BF16), peer_v[l].astype(BF16).T,
                    row(norm_final), final=(l == depth - 1))
    return x2d.reshape(bsz, length, d)
```

```python
import functools
import math

import jax
import jax.numpy as jnp
from jax import lax
from jax.experimental import pallas as pl
from jax.experimental.pallas import tpu as pltpu

F32 = jnp.float32
BF16 = jnp.bfloat16

D_MODEL = 1024
HEADS = 4
HEAD_DIM = 128
MIX = HEADS * HEAD_DIM
HG_F_MIN = 1e-6
S5_GROUPS = 32
S5_GROUP = 16
S5_STATE = 64
S5_WIDTH = S5_GROUPS * S5_GROUP
S5_LANES = S5_GROUPS * S5_STATE
S5_BLOCKS = 4
RET_CHUNK = 256
ROPE_BASE = 10000.0
IN_COLS = 4 * MIX + S5_WIDTH + 4 * MIX + 3 * D_MODEL
XA_HEADS = 4
XA_DH = D_MODEL // XA_HEADS
PEER_HEADS = 8
PEER_NKEYS = 128
PEER_DH = 128
PEER_TOPK = 16
EPS = 1e-6
NEG_INF = float("-inf")
UNRANKED = 99.0
BF16_ROWS = 16
PEER_CHUNK_BLOCKS = (4, 4, 4, 4)

VMEM_LIMIT = 56 * 1024 * 1024

NT_DIMS = (((1,), (1,)), ((), ()))
TN_DIMS = (((0,), (0,)), ((), ()))


def _params(sem):
    return pltpu.CompilerParams(dimension_semantics=sem, vmem_limit_bytes=VMEM_LIMIT)


def _rms(x, gain):
    return x * lax.rsqrt(jnp.mean(x * x, axis=-1, keepdims=True) + EPS) * gain


def _gelu(x):
    return 0.5 * x * (1.0 + jnp.tanh(math.sqrt(2.0 / math.pi) * (x + 0.044715 * (x * x * x))))


def _gelu2(x):
    c = math.sqrt(2.0 / math.pi)
    return x + x * jnp.tanh(x * (c + (c * 0.044715) * (x * x)))


def _sigmoid(x):
    return 1.0 / (1.0 + jnp.exp(-x))


def _silu(x):
    return x * _sigmoid(x)


def _dot(a, b):
    return jnp.dot(a, b, preferred_element_type=F32)


def _dot_nt(a, b):
    return lax.dot_general(a, b, NT_DIMS, preferred_element_type=F32)


def _dot_tn(a, b):
    return lax.dot_general(a, b, TN_DIMS, preferred_element_type=F32)


def _inproj_kernel(x_ref, g_ref, w_ref, o_ref, h_scr):
    @pl.when(pl.program_id(1) == 0)
    def _():
        h_scr[...] = _rms(x_ref[...], g_ref[...]).astype(BF16)

    o_ref[...] = _dot(h_scr[...], w_ref[...])


def _inproj(x2d, gain, w_layers, layer, tm=1024, tn=2560):
    m, d = x2d.shape
    n = w_layers.shape[2]
    return pl.pallas_call(
        _inproj_kernel,
        grid=(m // tm, n // tn),
        in_specs=[
            pl.BlockSpec((tm, d), lambda i, j: (i, 0)),
            pl.BlockSpec((1, d), lambda i, j: (0, 0)),
            pl.BlockSpec((None, d, tn), lambda i, j: (layer, 0, j)),
        ],
        out_specs=pl.BlockSpec((tm, tn), lambda i, j: (i, j)),
        out_shape=jax.ShapeDtypeStruct((m, n), F32),
        scratch_shapes=[pltpu.VMEM((tm, d), BF16)],
        compiler_params=_params(("parallel", "arbitrary")),
        name="inproj",
    )(x2d, gain, w_layers)


def _hgrn_kernel(layer, q_ref, f_ref, i_ref, g_ref, lb_ref, ng_ref, o_ref, st_scr, lvl_scr):
    tc = q_ref.shape[1]
    n_lvl = tc.bit_length() - 1

    @pl.when(pl.program_id(1) == 0)
    def _():
        st_scr[...] = jnp.zeros_like(st_scr)
        ti = lax.broadcasted_iota(jnp.int32, (tc, tc), 0)
        si = lax.broadcasted_iota(jnp.int32, (tc, tc), 1)
        top_bit = (lax.bitcast_convert_type((ti ^ si).astype(F32), jnp.int32) >> 23) - 127
        lvl_scr[...] = jnp.where(si < ti, top_bit, jnp.where(si == ti, n_lvl, -1))

    lbs = lb_ref[...]
    e = jnp.exp(lbs - jnp.max(lbs, axis=0, keepdims=True))
    sm = e / jnp.sum(e, axis=0, keepdims=True)
    lower = jnp.sum(sm[0:layer + 1], axis=0, keepdims=True) - sm[0:1]

    z = f_ref[0]
    fv = lower + (1.0 - lower) * _sigmoid(z)
    logf = jnp.log(jnp.clip(fv, HG_F_MIN, 1.0))
    kk = (1.0 - lower) * _sigmoid(-z)
    qs = q_ref[0] * HEAD_DIM ** -0.5
    vb = i_ref[0].astype(BF16)
    heads = [slice(HEAD_DIM * hh, HEAD_DIM * (hh + 1)) for hh in range(HEADS)]

    lvl = lvl_scr[...]
    qb = qs.astype(BF16)
    kb = kk.astype(BF16)
    att = [jnp.where(lvl == n_lvl, _dot_nt(qb[:, sl], kb[:, sl]), 0.0) for sl in heads]

    rowi = lax.broadcasted_iota(jnp.int32, (tc, MIX), 0)
    pre = logf
    post = jnp.zeros_like(logf)
    for level in range(n_lvl):
        h = 1 << level
        upper = (rowi & h) != 0
        x = (jnp.where(upper, qs, kk) * jnp.exp(jnp.where(upper, pre, post))).astype(BF16)
        here = lvl == level
        att = [jnp.where(here, _dot_nt(x[:, sl], x[:, sl]), a) for sl, a in zip(heads, att)]
        total = pre + post
        pre = pre + jnp.where(upper, pltpu.roll(total, h, 0), 0.0)
        post = post + jnp.where(upper, 0.0, pltpu.roll(total, tc - h, 0))

    qd = (qs * jnp.exp(pre)).astype(BF16)
    kd = (kk * jnp.exp(post)).astype(BF16)
    decay = jnp.exp(pre[tc - 1:tc, :])
    gate = g_ref[0]
    ng = ng_ref[...]
    outs = []
    for hh, sl in enumerate(heads):
        st = st_scr[hh]
        oh = _dot(att[hh].astype(BF16), vb[:, sl]) + _dot_nt(qd[:, sl], st.astype(BF16))
        st_scr[hh] = st * decay[:, sl] + _dot_tn(vb[:, sl], kd[:, sl])
        outs.append(oh * lax.rsqrt(jnp.mean(oh * oh, axis=-1, keepdims=True) + EPS) * ng[:, sl])
    o_ref[0] = (jnp.concatenate(outs, axis=1) * _silu(gate)).astype(o_ref.dtype)


def _hgrn(proj3, lb, ng, layer, tc=256):
    bsz, length, _ = proj3.shape
    tc = min(tc, length)
    col = lambda k: pl.BlockSpec((1, tc, MIX), lambda bi, ti, k=k: (bi, ti, k))
    return pl.pallas_call(
        functools.partial(_hgrn_kernel, layer),
        grid=(bsz, length // tc),
        in_specs=[col(0), col(1), col(2), col(3),
                  pl.BlockSpec(lb.shape, lambda bi, ti: (0, 0)),
                  pl.BlockSpec((1, MIX), lambda bi, ti: (0, 0))],
        out_specs=pl.BlockSpec((1, tc, MIX), lambda bi, ti: (bi, ti, 0)),
        out_shape=jax.ShapeDtypeStruct((bsz, length, MIX), BF16),
        scratch_shapes=[pltpu.VMEM((HEADS, HEAD_DIM, HEAD_DIM), F32), pltpu.VMEM((tc, tc), jnp.int32)],
        compiler_params=_params(("parallel", "arbitrary")),
        name="hgrn2",
    )(proj3, proj3, proj3, proj3, lb, ng)


def _s5_prep_kernel(lr_ref, li_ref, ls_ref, bre_ref, bim_ref, bbre_ref, bbim_ref, tab_ref):
    lr = jnp.minimum(lr_ref[...], -1e-4)
    li = li_ref[...]
    dt = jnp.exp(ls_ref[...])
    mag = jnp.exp(lr * dt)
    a_re = mag * jnp.cos(li * dt)
    a_im = mag * jnp.sin(li * dt)
    den = lr * lr + li * li
    z_re = ((a_re - 1.0) * lr + a_im * li) / den
    z_im = (a_im * lr - (a_re - 1.0) * li) / den
    bre = bre_ref[...]
    bim = bim_ref[...]
    bbre_ref[...] = z_re * bre - z_im * bim
    bbim_ref[...] = z_re * bim + z_im * bre
    row = lax.broadcasted_iota(jnp.int32, (8, a_re.shape[1]), 0)
    p_re = jnp.broadcast_to(a_re, row.shape)
    p_im = jnp.broadcast_to(a_im, row.shape)
    c_re, c_im = a_re, a_im
    powers = [(a_re, a_im)]
    for r in range(1, 8):
        c_re, c_im = c_re * a_re - c_im * a_im, c_re * a_im + c_im * a_re
        powers.append((c_re, c_im))
        p_re = jnp.where(row == r, c_re, p_re)
        p_im = jnp.where(row == r, c_im, p_im)
    for slot, d in enumerate((1, 2, 4)):
        tab_ref[2 * slot] = jnp.where(row >= d, powers[d - 1][0], 0.0)
        tab_ref[2 * slot + 1] = jnp.where(row >= d, powers[d - 1][1], 0.0)
    tab_ref[6] = p_re
    tab_ref[7] = p_im


def _s5_prep(lam_re, lam_im, log_step, b_re, b_im):
    lanes = S5_LANES
    row = lambda a: a.astype(F32).reshape(1, lanes)
    ls = jnp.repeat(log_step.astype(F32), S5_STATE).reshape(1, lanes)
    bt = lambda a: a.astype(F32).reshape(lanes, S5_GROUP).T
    full = lambda shape: pl.BlockSpec(shape, lambda: tuple(0 for _ in shape))
    return pl.pallas_call(
        _s5_prep_kernel,
        in_specs=[full((1, lanes))] * 3 + [full((S5_GROUP, lanes))] * 2,
        out_specs=[full((S5_GROUP, lanes))] * 2 + [full((8, 8, lanes))],
        out_shape=[jax.ShapeDtypeStruct((S5_GROUP, lanes), F32)] * 2
        + [jax.ShapeDtypeStruct((8, 8, lanes), F32)],
        name="s5_prep",
    )(row(lam_re), row(lam_im), ls, bt(b_re), bt(b_im))


def _s5_kernel(u_ref, bbd_ref, tab_ref, cre_ref, cim_ref, d_ref, wg_ref, bg_ref, o_ref,
               xr_scr, xi_scr, car_scr):
    tt = u_ref.shape[1]
    lanes = S5_LANES
    lw = 512

    @pl.when(pl.program_id(1) == 0)
    def _():
        car_scr[...] = jnp.zeros_like(car_scr)

    u = u_ref[0]
    ub = u.astype(BF16)
    n_blk = bbd_ref.shape[0]
    cw = S5_WIDTH // n_blk
    bl = lanes // n_blk
    for jb in range(n_blk):
        bu = _dot(ub[:, cw * jb:cw * (jb + 1)], bbd_ref[jb])
        xr_scr[:, bl * jb:bl * (jb + 1)] = bu[:, :bl]
        xi_scr[:, bl * jb:bl * (jb + 1)] = bu[:, bl:]

    for lc in range(lanes // lw):
        sl = slice(lw * lc, lw * (lc + 1))
        steps = [(d, tab_ref[2 * s][:, sl], tab_ref[2 * s + 1][:, sl]) for s, d in enumerate((1, 2, 4))]
        p_re = tab_ref[6][:, sl]
        p_im = tab_ref[7][:, sl]

        def body(k, carry, sl=sl, steps=steps, p_re=p_re, p_im=p_im):
            c_re, c_im = carry
            r0 = pl.multiple_of(k * 8, 8)
            xr = xr_scr[pl.ds(r0, 8), sl]
            xi = xi_scr[pl.ds(r0, 8), sl]
            for d, a_re, a_im in steps:
                sr = pltpu.roll(xr, d, 0)
                si = pltpu.roll(xi, d, 0)
                xr, xi = xr + a_re * sr - a_im * si, xi + a_re * si + a_im * sr
            xr, xi = xr + p_re * c_re - p_im * c_im, xi + p_re * c_im + p_im * c_re
            xr_scr[pl.ds(r0, 8), sl] = xr
            xi_scr[pl.ds(r0, 8), sl] = xi
            return xr[7:8, :], xi[7:8, :]

        c_re, c_im = lax.fori_loop(0, tt // 8, body, (car_scr[0:1, sl], car_scr[1:2, sl]), unroll=2)
        car_scr[0:1, sl] = c_re
        car_scr[1:2, sl] = c_im

    ys = []
    for jb in range(n_blk):
        cols = slice(bl * jb, bl * (jb + 1))
        ys.append(_dot(xr_scr[:, cols].astype(BF16), cre_ref[jb]) - _dot(xi_scr[:, cols].astype(BF16), cim_ref[jb]))
    y = jnp.concatenate(ys, axis=1) + d_ref[...] * u
    y = _gelu(y)
    o_ref[0] = (y * _sigmoid(_dot(y.astype(BF16), wg_ref[...]) + bg_ref[...])).astype(o_ref.dtype)


def _s5(proj3, bbd, tab, cre, cim, dskip, wg, bg, tt=256):
    bsz, length, _ = proj3.shape
    lanes = S5_LANES
    const = lambda shape: pl.BlockSpec(shape, lambda bi, ti: tuple(0 for _ in shape))
    return pl.pallas_call(
        _s5_kernel,
        grid=(bsz, length // tt),
        in_specs=[pl.BlockSpec((1, tt, S5_WIDTH), lambda bi, ti: (bi, ti, 4)),
                  const(bbd.shape), const((8, 8, lanes)), const(cre.shape), const(cim.shape),
                  const((1, S5_WIDTH)), const((S5_WIDTH, S5_WIDTH)), const((1, S5_WIDTH))],
        out_specs=pl.BlockSpec((1, tt, S5_WIDTH), lambda bi, ti: (bi, ti, 0)),
        out_shape=jax.ShapeDtypeStruct((bsz, length, S5_WIDTH), BF16),
        scratch_shapes=[pltpu.VMEM((tt, lanes), F32), pltpu.VMEM((tt, lanes), F32),
                        pltpu.VMEM((8, lanes), F32)],
        compiler_params=_params(("parallel", "arbitrary")),
        name="s5",
    )(proj3, bbd, tab, cre, cim, dskip, wg, bg)


def _block_diag_in(bb_t):
    g, c, p = S5_GROUPS, S5_GROUP, S5_STATE
    gb = g // S5_BLOCKS
    per_group = bb_t.reshape(c, S5_BLOCKS, gb, p).transpose(1, 2, 0, 3)
    eye = jnp.eye(gb, dtype=F32)
    full = eye[None, :, None, :, None] * per_group[:, :, :, None, :]
    return full.reshape(S5_BLOCKS, gb * c, gb * p)


def _block_diag_out(cmat):
    g, c, p = S5_GROUPS, S5_GROUP, S5_STATE
    gb = g // S5_BLOCKS
    per_group = cmat.astype(F32).reshape(S5_BLOCKS, gb, c, p).transpose(0, 1, 3, 2)
    eye = jnp.eye(gb, dtype=F32)
    full = eye[None, :, None, :, None] * per_group[:, :, :, None, :]
    return full.reshape(S5_BLOCKS, gb * p, gb * c)


def _rope_kernel(pos_ref, cos_ref, sin_ref):
    pos = pos_ref[0].astype(F32)
    half = HEAD_DIM // 2
    lane = lax.broadcasted_iota(jnp.int32, (1, HEAD_DIM), 1)
    j = (lane & (half - 1)).astype(F32)
    inv = jnp.exp(j * (-math.log(ROPE_BASE) / half))
    ang = pos * inv
    cos_ref[0] = jnp.cos(ang)
    sin_ref[0] = jnp.where(lane < half, -1.0, 1.0) * jnp.sin(ang)


def _rope_tables(positions, tr=1024):
    bsz, length = positions.shape
    tr = min(tr, length)
    spec_o = pl.BlockSpec((1, tr, HEAD_DIM), lambda bi, ti: (bi, ti, 0))
    return pl.pallas_call(
        _rope_kernel,
        grid=(bsz, length // tr),
        in_specs=[pl.BlockSpec((1, tr, 1), lambda bi, ti: (bi, ti, 0))],
        out_specs=[spec_o, spec_o],
        out_shape=[jax.ShapeDtypeStruct((bsz, length, HEAD_DIM), F32)] * 2,
        compiler_params=_params(("parallel", "parallel")),
        name="rope_tables",
    )(positions.reshape(bsz, length, 1))


def _ret_kernel(q_ref, k_ref, v_ref, g_ref, cos_ref, sin_ref, o_ref, s_scr):
    c = RET_CHUNK

    @pl.when(pl.program_id(1) == 0)
    def _():
        s_scr[...] = jnp.zeros_like(s_scr)

    cosf = cos_ref[0]
    sinf = sin_ref[0]
    rel = (lax.broadcasted_iota(jnp.int32, (c, c), 0) - lax.broadcasted_iota(jnp.int32, (c, c), 1)).astype(F32)
    idx = lax.broadcasted_iota(jnp.int32, (c, 1), 0).astype(F32)
    qa, ka, va, ga = q_ref[0], k_ref[0], v_ref[0], g_ref[0]
    outs = []
    for hh in range(HEADS):
        sl = slice(HEAD_DIM * hh, HEAD_DIM * (hh + 1))
        lg = math.log1p(-2.0 ** (-5.0 - hh))
        q = qa[:, sl]
        k = ka[:, sl]
        vb = va[:, sl].astype(BF16)
        qh = q * cosf + pltpu.roll(q, HEAD_DIM // 2, 1) * sinf
        kh = (k * cosf + pltpu.roll(k, HEAD_DIM // 2, 1) * sinf) * HEAD_DIM ** -0.5
        dmat = jnp.where(rel >= 0.0, jnp.exp(lg * jnp.maximum(rel, 0.0)), 0.0)
        scores = _dot_nt(qh.astype(BF16), kh.astype(BF16)) * dmat
        o = _dot(scores.astype(BF16), vb)
        q_dec = qh * jnp.exp(lg * (idx + 1.0))
        k_dec = kh * jnp.exp(lg * (c - 1.0 - idx))
        st = s_scr[hh]
        o = o + _dot(q_dec.astype(BF16), st.astype(BF16))
        s_scr[hh] = math.exp(lg * c) * st + _dot_tn(k_dec.astype(BF16), vb)
        mu = jnp.mean(o, axis=-1, keepdims=True)
        var = jnp.mean((o - mu) ** 2, axis=-1, keepdims=True)
        outs.append((o - mu) * lax.rsqrt(var + EPS))
    o_ref[0] = (jnp.concatenate(outs, axis=1) * _silu(ga)).astype(o_ref.dtype)


def _retention(proj3, cos_t, sin_t):
    bsz, length, _ = proj3.shape
    c = RET_CHUNK
    col = lambda k: pl.BlockSpec((1, c, MIX), lambda bi, ti, k=k: (bi, ti, k))
    tab = pl.BlockSpec((1, c, HEAD_DIM), lambda bi, ti: (bi, ti, 0))
    return pl.pallas_call(
        _ret_kernel,
        grid=(bsz, length // c),
        in_specs=[col(5), col(6), col(7), col(8), tab, tab],
        out_specs=pl.BlockSpec((1, c, MIX), lambda bi, ti: (bi, ti, 0)),
        out_shape=jax.ShapeDtypeStruct((bsz, length, MIX), BF16),
        scratch_shapes=[pltpu.VMEM((HEADS, HEAD_DIM, HEAD_DIM), F32)],
        compiler_params=_params(("parallel", "arbitrary")),
        name="retention",
    )(proj3, proj3, proj3, proj3, cos_t, sin_t)


def _memkv_kernel(m_ref, g_ref, w_ref, k_ref, v_ref):
    kv = _dot(_rms(m_ref[0], g_ref[...]).astype(BF16), w_ref[...])
    k_ref[0] = kv[:, :D_MODEL].astype(BF16)
    v_ref[0] = kv[:, D_MODEL:].astype(BF16)


def _memkv(mem, gain, wkv_bf16):
    bsz, m, d = mem.shape
    spec_o = pl.BlockSpec((1, m, d), lambda bi: (bi, 0, 0))
    return pl.pallas_call(
        _memkv_kernel,
        grid=(bsz,),
        in_specs=[pl.BlockSpec((1, m, d), lambda bi: (bi, 0, 0)),
                  pl.BlockSpec((1, d), lambda bi: (0, 0)),
                  pl.BlockSpec((d, 2 * d), lambda bi: (0, 0))],
        out_specs=[spec_o, spec_o],
        out_shape=[jax.ShapeDtypeStruct((bsz, m, d), BF16)] * 2,
        compiler_params=_params(("parallel",)),
        name="mem_kv",
    )(mem, gain, wkv_bf16)


def _mix_kernel(x_ref, ga_ref, gb_ref, bg_ref, hg_ref, s5_ref, rt_ref, wb_ref, wout_ref,
                nc_ref, k_ref, v_ref, wq_ref, wo_ref, o_ref):
    d = D_MODEL
    logits = jnp.concatenate([ga_ref[...], gb_ref[...]], axis=1) + bg_ref[...]
    gates = _sigmoid(logits)
    merged = (gates[:, 0:d] * _dot(hg_ref[...], wb_ref[0:MIX, :])
              + gates[:, d:2 * d] * _dot(s5_ref[...], wb_ref[MIX:MIX + S5_WIDTH, :])
              + gates[:, 2 * d:3 * d] * _dot(rt_ref[...], wb_ref[MIX + S5_WIDTH:, :]))
    x1 = x_ref[...] + _dot(merged.astype(BF16), wout_ref[...])

    hc = _rms(x1, nc_ref[...]).astype(BF16)
    q = _dot(hc, wq_ref[...])
    kk = k_ref[0]
    vv = v_ref[0]
    outs = []
    for hh in range(XA_HEADS):
        sl = slice(XA_DH * hh, XA_DH * (hh + 1))
        s = _dot_nt(q[:, sl].astype(BF16), kk[:, sl]) * XA_DH ** -0.5
        p = jnp.exp(s - jnp.max(s, axis=-1, keepdims=True))
        p = p / jnp.sum(p, axis=-1, keepdims=True)
        outs.append(_dot(p.astype(BF16), vv[:, sl]))
    attn = jnp.concatenate(outs, axis=1)
    o_ref[...] = x1 + _dot(attn.astype(BF16), wo_ref[...])


def _mix(x2d, proj, bgate, o_hg, o_s5, o_rt, wb, wout, ncross, kmem, vmem, wq, wo, length, tm=512):
    m, d = x2d.shape
    per_batch = length // tm
    tok = lambda w: pl.BlockSpec((tm, w), lambda i: (i, 0))
    const = lambda shape: pl.BlockSpec(shape, lambda i: tuple(0 for _ in shape))
    gw = 1536
    mem_spec = pl.BlockSpec((1,) + kmem.shape[1:], lambda i: (i // per_batch, 0, 0))
    return pl.pallas_call(
        _mix_kernel,
        grid=(m // tm,),
        in_specs=[tok(d),
                  pl.BlockSpec((tm, gw), lambda i: (i, 3)), pl.BlockSpec((tm, gw), lambda i: (i, 4)),
                  const((1, 3 * d)), tok(MIX), tok(S5_WIDTH), tok(MIX),
                  const(wb.shape), const((d, d)), const((1, d)), mem_spec, mem_spec,
                  const((d, d)), const((d, d))],
        out_specs=tok(d),
        out_shape=jax.ShapeDtypeStruct((m, d), F32),
        compiler_params=_params(("parallel",)),
        name="merge_xattn",
    )(x2d, proj, proj, bgate, o_hg, o_s5, o_rt, wb, wout, ncross, kmem, vmem, wq, wo)


def _sorting_network(n):
    pairs = []
    p = 1
    while p < n:
        k = p
        while k >= 1:
            for j in range(k % p, n - k, 2 * k):
                for i in range(min(k, n - j - k)):
                    if (i + j) // (2 * p) == (i + j + k) // (2 * p):
                        pairs.append((i + j, i + j + k))
            k //= 2
        p *= 2
    return pairs


def _top_values(s):
    n_keys, t = s.shape
    slabs = [s[8 * i:8 * (i + 1), :] for i in range(n_keys // 8)]
    for lo, hi in _sorting_network(len(slabs)):
        slabs[lo], slabs[hi] = jnp.maximum(slabs[lo], slabs[hi]), jnp.minimum(slabs[lo], slabs[hi])
    row16 = lax.broadcasted_iota(jnp.int32, (PEER_TOPK, t), 0)
    vals = jnp.zeros((PEER_TOPK, t), F32)
    for r in range(PEER_TOPK):
        m = jnp.max(slabs[0], axis=0, keepdims=True)
        hit = slabs[0] == m
        for i in range(PEER_TOPK - 1 - r):
            slabs[i] = jnp.where(hit, slabs[i + 1], slabs[i])
        vals = jnp.where(row16 == r, m, vals)
    return vals


def _route_kernel(x_ref, g_ref, wqt_ref, keys_ref, h_ref, rank2_ref, e2_ref, n1_ref, e1_ref):
    t = x_ref.shape[0]
    k = PEER_TOPK
    hb = _rms(x_ref[...], g_ref[...]).T.astype(BF16)
    h_ref[...] = hb
    qt = _dot(wqt_ref[...], hb).astype(BF16)
    row16 = lax.broadcasted_iota(jnp.int32, (k, t), 0)
    row8 = lax.broadcasted_iota(jnp.int32, (8, t), 0)
    zeros8 = jnp.zeros((8, t), F32)
    for hd in range(PEER_HEADS):
        base = hd * 2 * PEER_DH
        s1 = _dot(keys_ref[2 * hd], qt[base:base + PEER_DH])
        s2 = _dot(keys_ref[2 * hd + 1], qt[base + PEER_DH:base + 2 * PEER_DH])
        a = _top_values(s1)
        b = _top_values(s2)
        rank2 = jnp.full(s2.shape, UNRANKED, F32)
        for jr in range(k, 0, -1):
            rank2 = jnp.where(s2 >= b[jr - 1:jr], float(jr), rank2)
        pieces = [a + b[0:1]]
        for j in range(2, 9):
            pieces.append(jnp.where(row8 < k // j, a[0:8] + b[j - 1:j], NEG_INF))
        pieces.append(a[0:1] + b[8:16])
        cand = jnp.concatenate(pieces, axis=0)
        top = a[0:1] + b[0:1]
        work = cand
        thr = top
        for r in range(k):
            thr = jnp.max(work, axis=0, keepdims=True)
            work = jnp.where(work == thr, NEG_INF, work)
        sel = cand >= thr
        zsum = jnp.sum(jnp.where(sel, jnp.exp(cand - top), 0.0), axis=0, keepdims=True)
        picked = sel.astype(F32)
        low = picked[16:24]
        for j in range(3, 9):
            low = low + picked[8 * j:8 * j + 8]
        cnt_tail = jnp.sum(picked[72:80], axis=0, keepdims=True)
        n_of_i = picked[0:16] + jnp.concatenate([low, zeros8], axis=0) + jnp.where(row16 == 0, cnt_tail, 0.0)
        n1 = jnp.zeros(s1.shape, F32)
        for i in range(k):
            n1 = jnp.where(s1 == a[i:i + 1], n_of_i[i:i + 1], n1)
        rows = slice(hd * PEER_NKEYS, (hd + 1) * PEER_NKEYS)
        rank2_ref[rows, :] = rank2.astype(BF16)
        e2_ref[rows, :] = (0.5 * jnp.exp(s2 - b[0:1]) / zsum).astype(BF16)
        n1_ref[rows, :] = n1
        e1_ref[rows, :] = jnp.exp(s1 - a[0:1])


def _route(x2d, gain, wqt, keys2, tt=256):
    m, d = x2d.shape
    rows = PEER_HEADS * PEER_NKEYS
    tab = pl.BlockSpec((rows, tt), lambda i: (0, i))
    return pl.pallas_call(
        _route_kernel,
        grid=(m // tt,),
        in_specs=[pl.BlockSpec((tt, d), lambda i: (i, 0)),
                  pl.BlockSpec((1, d), lambda i: (0, 0)),
                  pl.BlockSpec(wqt.shape, lambda i: (0, 0)),
                  pl.BlockSpec(keys2.shape, lambda i: (0, 0, 0))],
        out_specs=[pl.BlockSpec((d, tt), lambda i: (0, i)), tab, tab, tab, tab],
        out_shape=[jax.ShapeDtypeStruct((d, m), BF16)] + [jax.ShapeDtypeStruct((rows, m), BF16)] * 2
        + [jax.ShapeDtypeStruct((rows, m), F32)] * 2,
        compiler_params=_params(("parallel",)),
        name="peer_route",
    )(x2d, gain, wqt, keys2)


def _peer_kernel(final, x_ref, h_ref, rank2_ref, e2_ref, n1_ref, e1_ref, u_ref, vt_ref, gf_ref,
                 o_ref, acc_scr):
    j = pl.program_id(1)
    eb = u_ref.shape[0]
    nk = PEER_NKEYS

    @pl.when(j == 0)
    def _():
        acc_scr[...] = jnp.zeros_like(acc_scr)

    tt = h_ref.shape[1]
    half = tt // 2
    i1_base = pl.multiple_of(j * (eb // nk), eb // nk)
    sizes = [s for s in PEER_CHUNK_BLOCKS]
    assert sum(sizes) * nk == eb
    starts = [sum(sizes[:c]) for c in range(len(sizes))]
    rows_of = [slice(st * nk, (st + sz) * nk) for st, sz in zip(starts, sizes)]

    def score(ck, th):
        return jnp.dot(u_ref[rows_of[ck], :], h_ref[:, th * half:(th + 1) * half],
                       preferred_element_type=F32).astype(BF16)

    def key_row(ref, hd, il, cols):
        r = ref[pl.ds(hd * nk + i1_base + il, 1), :][:, cols]
        return jnp.tile(jnp.broadcast_to(r, (BF16_ROWS, half)).astype(BF16), (nk // BF16_ROWS, 1))

    def act_tile(z_half, ck, sub, th):
        il = starts[ck] + sub
        cols = slice(th * half, (th + 1) * half)
        gate = None
        for hd in range(PEER_HEADS):
            rows = slice(hd * nk, (hd + 1) * nk)
            sel = jnp.where(rank2_ref[rows, cols] <= key_row(n1_ref, hd, il, cols), e2_ref[rows, cols], 0.0)
            term = sel * key_row(e1_ref, hd, il, cols)
            gate = term if gate is None else gate + term
        return _gelu2(z_half[sub * nk:(sub + 1) * nk, :]) * gate

    def apply(ck, th, act):
        cols = slice(th * half, (th + 1) * half)
        acc_scr[:, cols] += _dot(vt_ref[:, rows_of[ck]], act)

    z = [score(0, 0), score(0, 1)]
    prev = None
    for ck, n_sub in enumerate(sizes):
        pieces = []
        z_next = [None, None]
        if ck + 1 < len(sizes):
            pieces += [("score", 0), ("score", 1)]
        if prev is not None:
            pieces += [("apply", 0), ("apply", 1)]
        acts = ([], [])
        for sub in range(n_sub):
            for th in range(2):
                acts[th].append(act_tile(z[th], ck, sub, th))
            for kind, th in pieces[sub * len(pieces) // n_sub:(sub + 1) * len(pieces) // n_sub]:
                if kind == "score":
                    z_next[th] = score(ck + 1, th)
                else:
                    apply(ck - 1, th, prev[th])
        prev = [jnp.concatenate(acts[0], axis=0), jnp.concatenate(acts[1], axis=0)]
        z = z_next
    apply(len(sizes) - 1, 0, prev[0])
    apply(len(sizes) - 1, 1, prev[1])

    @pl.when(j == pl.num_programs(1) - 1)
    def _():
        y = x_ref[...] + acc_scr[...].T
        if final:
            y = _rms(y, gf_ref[...])
        o_ref[...] = y


def _peer(x2d, h, rank2, e2, n1, e1, u_layers, vt_layers, layer, gfinal, final, tt=512, eb=2048):
    m, d = x2d.shape
    ne = u_layers.shape[1]
    rows = PEER_HEADS * PEER_NKEYS
    tab = pl.BlockSpec((rows, tt), lambda i, j: (0, i))
    return pl.pallas_call(
        functools.partial(_peer_kernel, final),
        grid=(m // tt, ne // eb),
        in_specs=[pl.BlockSpec((tt, d), lambda i, j: (i, 0)),
                  pl.BlockSpec((d, tt), lambda i, j: (0, i)),
                  tab, tab, tab, tab,
                  pl.BlockSpec((None, eb, d), lambda i, j: (layer, j, 0)),
                  pl.BlockSpec((None, d, eb), lambda i, j: (layer, 0, j)),
                  pl.BlockSpec((1, d), lambda i, j: (0, 0))],
        out_specs=pl.BlockSpec((tt, d), lambda i, j: (i, 0)),
        out_shape=jax.ShapeDtypeStruct((m, d), F32),
        scratch_shapes=[pltpu.VMEM((d, tt), F32)],
        compiler_params=_params(("parallel", "arbitrary")),
        name="peer_experts",
    )(x2d, h, rank2, e2, n1, e1, u_layers, vt_layers, gfinal)


def kernel(x, mem, positions, norm_mix, w_in, b_gate, hgrn_lb, hgrn_norm, s5_lambda_re, s5_lambda_im,
           s5_log_step, s5_b_re, s5_b_im, s5_c_re, s5_c_im, s5_d, s5_w_glu, s5_b_glu, w_branch, w_out,
           norm_cross, norm_mem, xa_wq, xa_wkv, xa_wo, norm_ffn, peer_wq, peer_keys, peer_u, peer_v,
           norm_final):
    bsz, length, d = x.shape
    depth = w_in.shape[0]
    row = lambda a: a.astype(F32).reshape(1, -1)
    x2d = x.astype(F32).reshape(bsz * length, d)
    cos_t, sin_t = _rope_tables(positions)
    lb = hgrn_lb.astype(F32)
    w_in_b = w_in.astype(BF16)
    peer_u_b = peer_u.astype(BF16)
    peer_vt_b = peer_v.astype(BF16).transpose(0, 2, 1)
    for l in range(depth):
        proj = _inproj(x2d, row(norm_mix[l]), w_in_b, l)
        proj3 = proj.reshape(bsz, length, IN_COLS)
        o_hg = _hgrn(proj3, lb, row(hgrn_norm[l]), l)
        bb_re, bb_im, tab = _s5_prep(s5_lambda_re[l], s5_lambda_im[l], s5_log_step[l], s5_b_re[l], s5_b_im[l])
        bbd = jnp.concatenate([_block_diag_in(bb_re), _block_diag_in(bb_im)], axis=2).astype(BF16)
        o_s5 = _s5(proj3, bbd, tab, _block_diag_out(s5_c_re[l]).astype(BF16),
                   _block_diag_out(s5_c_im[l]).astype(BF16), row(s5_d[l]),
                   s5_w_glu[l].astype(BF16), row(s5_b_glu[l]))
        o_rt = _retention(proj3, cos_t, sin_t)
        kmem, vmem = _memkv(mem.astype(F32), row(norm_mem[l]), xa_wkv[l].astype(BF16))
        x2d = _mix(x2d, proj, row(b_gate[l]), o_hg.reshape(-1, MIX), o_s5.reshape(-1, S5_WIDTH),
                   o_rt.reshape(-1, MIX), w_branch[l].astype(BF16), w_out[l].astype(BF16),
                   row(norm_cross[l]), kmem, vmem, xa_wq[l].astype(BF16), xa_wo[l].astype(BF16), length)
        keys2 = peer_keys[l].astype(BF16).reshape(PEER_HEADS * 2, PEER_NKEYS, PEER_DH)
        h, rank2, e2, n1, e1 = _route(x2d, row(norm_ffn[l]), peer_wq[l].astype(BF16).T, keys2)
        x2d = _peer(x2d, h, rank2, e2, n1, e1, peer_u_b, peer_vt_b, l, row(norm_final), final=(l == depth - 1))
    return x2d.reshape(bsz, length, d)
```

```python
import functools
import math

import jax
import jax.numpy as jnp
from jax import lax
from jax.experimental import pallas as pl
from jax.experimental.pallas import tpu as pltpu

F32 = jnp.float32
BF16 = jnp.bfloat16

D_MODEL = 1024
HEADS = 4
HEAD_DIM = 128
MIX = HEADS * HEAD_DIM
HG_F_MIN = 1e-6
S5_GROUPS = 32
S5_GROUP = 16
S5_STATE = 64
S5_WIDTH = S5_GROUPS * S5_GROUP
S5_LANES = S5_GROUPS * S5_STATE
S5_BLOCKS = 4
RET_CHUNK = 256
ROPE_BASE = 10000.0
IN_COLS = 4 * MIX + S5_WIDTH + 4 * MIX + 3 * D_MODEL
XA_HEADS = 4
XA_DH = D_MODEL // XA_HEADS
PEER_HEADS = 8
PEER_NKEYS = 128
PEER_DH = 128
PEER_TOPK = 16
EPS = 1e-6
NEG_INF = float("-inf")
UNRANKED = 99.0
BF16_ROWS = 16
PEER_CHUNK_BLOCKS = (4, 4, 4, 4)

VMEM_LIMIT = 56 * 1024 * 1024

NT_DIMS = (((1,), (1,)), ((), ()))
TN_DIMS = (((0,), (0,)), ((), ()))


def _params(sem):
    return pltpu.CompilerParams(dimension_semantics=sem, vmem_limit_bytes=VMEM_LIMIT)


def _rms(x, gain):
    return x * lax.rsqrt(jnp.mean(x * x, axis=-1, keepdims=True) + EPS) * gain


def _gelu(x):
    return 0.5 * x * (1.0 + jnp.tanh(math.sqrt(2.0 / math.pi) * (x + 0.044715 * (x * x * x))))


def _gelu2(x):
    c = math.sqrt(2.0 / math.pi)
    return x + x * jnp.tanh(x * (c + (c * 0.044715) * (x * x)))


def _sigmoid(x):
    return 1.0 / (1.0 + jnp.exp(-x))


def _silu(x):
    return x * _sigmoid(x)


def _dot(a, b):
    return jnp.dot(a, b, preferred_element_type=F32)


def _dot_nt(a, b):
    return lax.dot_general(a, b, NT_DIMS, preferred_element_type=F32)


def _dot_tn(a, b):
    return lax.dot_general(a, b, TN_DIMS, preferred_element_type=F32)


def _inproj_kernel(x_ref, g_ref, w_ref, o_ref, h_scr):
    @pl.when(pl.program_id(1) == 0)
    def _():
        h_scr[...] = _rms(x_ref[...], g_ref[...]).astype(BF16)

    o_ref[...] = _dot(h_scr[...], w_ref[...])


def _inproj(x2d, gain, w_layers, layer, tm=1024, tn=2560):
    m, d = x2d.shape
    n = w_layers.shape[2]
    return pl.pallas_call(
        _inproj_kernel,
        grid=(m // tm, n // tn),
        in_specs=[
            pl.BlockSpec((tm, d), lambda i, j: (i, 0)),
            pl.BlockSpec((1, d), lambda i, j: (0, 0)),
            pl.BlockSpec((None, d, tn), lambda i, j: (layer, 0, j)),
        ],
        out_specs=pl.BlockSpec((tm, tn), lambda i, j: (i, j)),
        out_shape=jax.ShapeDtypeStruct((m, n), F32),
        scratch_shapes=[pltpu.VMEM((tm, d), BF16)],
        compiler_params=_params(("parallel", "arbitrary")),
        name="inproj",
    )(x2d, gain, w_layers)


def _hgrn_kernel(layer, q_ref, f_ref, i_ref, g_ref, lb_ref, ng_ref, o_ref, st_scr, lvl_scr):
    tc = q_ref.shape[1]
    n_lvl = tc.bit_length() - 1

    @pl.when(pl.program_id(1) == 0)
    def _():
        st_scr[...] = jnp.zeros_like(st_scr)
        ti = lax.broadcasted_iota(jnp.int32, (tc, tc), 0)
        si = lax.broadcasted_iota(jnp.int32, (tc, tc), 1)
        top_bit = (lax.bitcast_convert_type((ti ^ si).astype(F32), jnp.int32) >> 23) - 127
        lvl_scr[...] = jnp.where(si < ti, top_bit, jnp.where(si == ti, n_lvl, -1))

    lbs = lb_ref[...]
    e = jnp.exp(lbs - jnp.max(lbs, axis=0, keepdims=True))
    sm = e / jnp.sum(e, axis=0, keepdims=True)
    lower = jnp.sum(sm[0:layer + 1], axis=0, keepdims=True) - sm[0:1]

    z = f_ref[0]
    fv = lower + (1.0 - lower) * _sigmoid(z)
    logf = jnp.log(jnp.clip(fv, HG_F_MIN, 1.0))
    kk = (1.0 - lower) * _sigmoid(-z)
    qs = q_ref[0] * HEAD_DIM ** -0.5
    vb = i_ref[0].astype(BF16)
    heads = [slice(HEAD_DIM * hh, HEAD_DIM * (hh + 1)) for hh in range(HEADS)]

    lvl = lvl_scr[...]
    qb = qs.astype(BF16)
    kb = kk.astype(BF16)
    att = [jnp.where(lvl == n_lvl, _dot_nt(qb[:, sl], kb[:, sl]), 0.0) for sl in heads]

    rowi = lax.broadcasted_iota(jnp.int32, (tc, MIX), 0)
    pre = logf
    post = jnp.zeros_like(logf)
    for level in range(n_lvl):
        h = 1 << level
        upper = (rowi & h) != 0
        x = (jnp.where(upper, qs, kk) * jnp.exp(jnp.where(upper, pre, post))).astype(BF16)
        here = lvl == level
        att = [jnp.where(here, _dot_nt(x[:, sl], x[:, sl]), a) for sl, a in zip(heads, att)]
        total = pre + post
        pre = pre + jnp.where(upper, pltpu.roll(total, h, 0), 0.0)
        post = post + jnp.where(upper, 0.0, pltpu.roll(total, tc - h, 0))

    qd = (qs * jnp.exp(pre)).astype(BF16)
    kd = (kk * jnp.exp(post)).astype(BF16)
    decay = jnp.exp(pre[tc - 1:tc, :])
    gate = g_ref[0]
    ng = ng_ref[...]
    outs = []
    for hh, sl in enumerate(heads):
        st = st_scr[hh]
        oh = _dot(att[hh].astype(BF16), vb[:, sl]) + _dot_nt(qd[:, sl], st.astype(BF16))
        st_scr[hh] = st * decay[:, sl] + _dot_tn(vb[:, sl], kd[:, sl])
        outs.append(oh * lax.rsqrt(jnp.mean(oh * oh, axis=-1, keepdims=True) + EPS) * ng[:, sl])
    o_ref[0] = (jnp.concatenate(outs, axis=1) * _silu(gate)).astype(o_ref.dtype)


def _s5_prep_kernel(lr_ref, li_ref, ls_ref, bre_ref, bim_ref, bbre_ref, bbim_ref, tab_ref):
    lr = jnp.minimum(lr_ref[...], -1e-4)
    li = li_ref[...]
    dt = jnp.exp(ls_ref[...])
    mag = jnp.exp(lr * dt)
    a_re = mag * jnp.cos(li * dt)
    a_im = mag * jnp.sin(li * dt)
    den = lr * lr + li * li
    z_re = ((a_re - 1.0) * lr + a_im * li) / den
    z_im = (a_im * lr - (a_re - 1.0) * li) / den
    bre = bre_ref[...]
    bim = bim_ref[...]
    bbre_ref[...] = z_re * bre - z_im * bim
    bbim_ref[...] = z_re * bim + z_im * bre
    row = lax.broadcasted_iota(jnp.int32, (8, a_re.shape[1]), 0)
    p_re = jnp.broadcast_to(a_re, row.shape)
    p_im = jnp.broadcast_to(a_im, row.shape)
    c_re, c_im = a_re, a_im
    powers = [(a_re, a_im)]
    for r in range(1, 8):
        c_re, c_im = c_re * a_re - c_im * a_im, c_re * a_im + c_im * a_re
        powers.append((c_re, c_im))
        p_re = jnp.where(row == r, c_re, p_re)
        p_im = jnp.where(row == r, c_im, p_im)
    for slot, d in enumerate((1, 2, 4)):
        tab_ref[2 * slot] = jnp.where(row >= d, powers[d - 1][0], 0.0)
        tab_ref[2 * slot + 1] = jnp.where(row >= d, powers[d - 1][1], 0.0)
    tab_ref[6] = p_re
    tab_ref[7] = p_im


def _s5_prep(lam_re, lam_im, log_step, b_re, b_im):
    lanes = S5_LANES
    row = lambda a: a.astype(F32).reshape(1, lanes)
    ls = jnp.repeat(log_step.astype(F32), S5_STATE).reshape(1, lanes)
    bt = lambda a: a.astype(F32).reshape(lanes, S5_GROUP).T
    full = lambda shape: pl.BlockSpec(shape, lambda: tuple(0 for _ in shape))
    return pl.pallas_call(
        _s5_prep_kernel,
        in_specs=[full((1, lanes))] * 3 + [full((S5_GROUP, lanes))] * 2,
        out_specs=[full((S5_GROUP, lanes))] * 2 + [full((8, 8, lanes))],
        out_shape=[jax.ShapeDtypeStruct((S5_GROUP, lanes), F32)] * 2
        + [jax.ShapeDtypeStruct((8, 8, lanes), F32)],
        name="s5_prep",
    )(row(lam_re), row(lam_im), ls, bt(b_re), bt(b_im))


def _s5_kernel(u_ref, bbd_ref, tab_ref, cre_ref, cim_ref, d_ref, wg_ref, bg_ref, o_ref,
               xr_scr, xi_scr, car_scr):
    tt = u_ref.shape[1]
    lanes = S5_LANES
    lw = 512

    @pl.when(pl.program_id(1) == 0)
    def _():
        car_scr[...] = jnp.zeros_like(car_scr)

    u = u_ref[0]
    ub = u.astype(BF16)
    n_blk = bbd_ref.shape[0]
    cw = S5_WIDTH // n_blk
    bl = lanes // n_blk
    for jb in range(n_blk):
        bu = _dot(ub[:, cw * jb:cw * (jb + 1)], bbd_ref[jb])
        xr_scr[:, bl * jb:bl * (jb + 1)] = bu[:, :bl]
        xi_scr[:, bl * jb:bl * (jb + 1)] = bu[:, bl:]

    for lc in range(lanes // lw):
        sl = slice(lw * lc, lw * (lc + 1))
        steps = [(d, tab_ref[2 * s][:, sl], tab_ref[2 * s + 1][:, sl]) for s, d in enumerate((1, 2, 4))]
        p_re = tab_ref[6][:, sl]
        p_im = tab_ref[7][:, sl]

        def body(k, carry, sl=sl, steps=steps, p_re=p_re, p_im=p_im):
            c_re, c_im = carry
            r0 = pl.multiple_of(k * 8, 8)
            xr = xr_scr[pl.ds(r0, 8), sl]
            xi = xi_scr[pl.ds(r0, 8), sl]
            for d, a_re, a_im in steps:
                sr = pltpu.roll(xr, d, 0)
                si = pltpu.roll(xi, d, 0)
                xr, xi = xr + a_re * sr - a_im * si, xi + a_re * si + a_im * sr
            xr, xi = xr + p_re * c_re - p_im * c_im, xi + p_re * c_im + p_im * c_re
            xr_scr[pl.ds(r0, 8), sl] = xr
            xi_scr[pl.ds(r0, 8), sl] = xi
            return xr[7:8, :], xi[7:8, :]

        c_re, c_im = lax.fori_loop(0, tt // 8, body, (car_scr[0:1, sl], car_scr[1:2, sl]), unroll=True)
        car_scr[0:1, sl] = c_re
        car_scr[1:2, sl] = c_im

    ys = []
    for jb in range(n_blk):
        cols = slice(bl * jb, bl * (jb + 1))
        ys.append(_dot(xr_scr[:, cols].astype(BF16), cre_ref[jb]) - _dot(xi_scr[:, cols].astype(BF16), cim_ref[jb]))
    y = jnp.concatenate(ys, axis=1) + d_ref[...] * u
    y = _gelu(y)
    o_ref[0] = (y * _sigmoid(_dot(y.astype(BF16), wg_ref[...]) + bg_ref[...])).astype(o_ref.dtype)


def _s5(proj3, bbd, tab, cre, cim, dskip, wg, bg, tt=256):
    bsz, length, _ = proj3.shape
    lanes = S5_LANES
    const = lambda shape: pl.BlockSpec(shape, lambda bi, ti: tuple(0 for _ in shape))
    return pl.pallas_call(
        _s5_kernel,
        grid=(bsz, length // tt),
        in_specs=[pl.BlockSpec((1, tt, S5_WIDTH), lambda bi, ti: (bi, ti, 4)),
                  const(bbd.shape), const((8, 8, lanes)), const(cre.shape), const(cim.shape),
                  const((1, S5_WIDTH)), const((S5_WIDTH, S5_WIDTH)), const((1, S5_WIDTH))],
        out_specs=pl.BlockSpec((1, tt, S5_WIDTH), lambda bi, ti: (bi, ti, 0)),
        out_shape=jax.ShapeDtypeStruct((bsz, length, S5_WIDTH), BF16),
        scratch_shapes=[pltpu.VMEM((tt, lanes), F32), pltpu.VMEM((tt, lanes), F32),
                        pltpu.VMEM((8, lanes), F32)],
        compiler_params=_params(("parallel", "arbitrary")),
        name="s5",
    )(proj3, bbd, tab, cre, cim, dskip, wg, bg)


def _block_diag_in(bb_t):
    g, c, p = S5_GROUPS, S5_GROUP, S5_STATE
    gb = g // S5_BLOCKS
    per_group = bb_t.reshape(c, S5_BLOCKS, gb, p).transpose(1, 2, 0, 3)
    eye = jnp.eye(gb, dtype=F32)
    full = eye[None, :, None, :, None] * per_group[:, :, :, None, :]
    return full.reshape(S5_BLOCKS, gb * c, gb * p)


def _block_diag_out(cmat):
    g, c, p = S5_GROUPS, S5_GROUP, S5_STATE
    gb = g // S5_BLOCKS
    per_group = cmat.astype(F32).reshape(S5_BLOCKS, gb, c, p).transpose(0, 1, 3, 2)
    eye = jnp.eye(gb, dtype=F32)
    full = eye[None, :, None, :, None] * per_group[:, :, :, None, :]
    return full.reshape(S5_BLOCKS, gb * p, gb * c)


def _rope_kernel(pos_ref, cos_ref, sin_ref):
    pos = pos_ref[0].astype(F32)
    half = HEAD_DIM // 2
    lane = lax.broadcasted_iota(jnp.int32, (1, HEAD_DIM), 1)
    j = (lane & (half - 1)).astype(F32)
    inv = jnp.exp(j * (-math.log(ROPE_BASE) / half))
    ang = pos * inv
    cos_ref[0] = jnp.cos(ang)
    sin_ref[0] = jnp.where(lane < half, -1.0, 1.0) * jnp.sin(ang)


def _rope_tables(positions, tr=1024):
    bsz, length = positions.shape
    tr = min(tr, length)
    spec_o = pl.BlockSpec((1, tr, HEAD_DIM), lambda bi, ti: (bi, ti, 0))
    return pl.pallas_call(
        _rope_kernel,
        grid=(bsz, length // tr),
        in_specs=[pl.BlockSpec((1, tr, 1), lambda bi, ti: (bi, ti, 0))],
        out_specs=[spec_o, spec_o],
        out_shape=[jax.ShapeDtypeStruct((bsz, length, HEAD_DIM), F32)] * 2,
        compiler_params=_params(("parallel", "parallel")),
        name="rope_tables",
    )(positions.reshape(bsz, length, 1))


def _ret_kernel(q_ref, k_ref, v_ref, g_ref, cos_ref, sin_ref, o_ref, s_scr):
    c = RET_CHUNK

    @pl.when(pl.program_id(1) == 0)
    def _():
        s_scr[...] = jnp.zeros_like(s_scr)

    cosf = cos_ref[0]
    sinf = sin_ref[0]
    rel = (lax.broadcasted_iota(jnp.int32, (c, c), 0) - lax.broadcasted_iota(jnp.int32, (c, c), 1)).astype(F32)
    idx = lax.broadcasted_iota(jnp.int32, (c, 1), 0).astype(F32)
    qa, ka, va, ga = q_ref[0], k_ref[0], v_ref[0], g_ref[0]
    outs = []
    for hh in range(HEADS):
        sl = slice(HEAD_DIM * hh, HEAD_DIM * (hh + 1))
        lg = math.log1p(-2.0 ** (-5.0 - hh))
        q = qa[:, sl]
        k = ka[:, sl]
        vb = va[:, sl].astype(BF16)
        qh = q * cosf + pltpu.roll(q, HEAD_DIM // 2, 1) * sinf
        kh = (k * cosf + pltpu.roll(k, HEAD_DIM // 2, 1) * sinf) * HEAD_DIM ** -0.5
        dmat = jnp.where(rel >= 0.0, jnp.exp(lg * jnp.maximum(rel, 0.0)), 0.0)
        scores = _dot_nt(qh.astype(BF16), kh.astype(BF16)) * dmat
        o = _dot(scores.astype(BF16), vb)
        q_dec = qh * jnp.exp(lg * (idx + 1.0))
        k_dec = kh * jnp.exp(lg * (c - 1.0 - idx))
        st = s_scr[hh]
        o = o + _dot(q_dec.astype(BF16), st.astype(BF16))
        s_scr[hh] = math.exp(lg * c) * st + _dot_tn(k_dec.astype(BF16), vb)
        mu = jnp.mean(o, axis=-1, keepdims=True)
        var = jnp.mean((o - mu) ** 2, axis=-1, keepdims=True)
        outs.append((o - mu) * lax.rsqrt(var + EPS))
    o_ref[0] = (jnp.concatenate(outs, axis=1) * _silu(ga)).astype(o_ref.dtype)


def _hgrn_ret_kernel(layer, q_ref, f_ref, i_ref, g_ref, lb_ref, ng_ref, rq_ref, rk_ref, rv_ref, rg_ref,
                     cos_ref, sin_ref, ohg_ref, ort_ref, st_scr, lvl_scr, s_scr):
    _hgrn_kernel(layer, q_ref, f_ref, i_ref, g_ref, lb_ref, ng_ref, ohg_ref, st_scr, lvl_scr)
    _ret_kernel(rq_ref, rk_ref, rv_ref, rg_ref, cos_ref, sin_ref, ort_ref, s_scr)


def _hgrn_retention(proj3, lb, ng, layer, cos_t, sin_t):
    bsz, length, _ = proj3.shape
    c = RET_CHUNK
    col = lambda k: pl.BlockSpec((1, c, MIX), lambda bi, ti, k=k: (bi, ti, k))
    tab = pl.BlockSpec((1, c, HEAD_DIM), lambda bi, ti: (bi, ti, 0))
    out = pl.BlockSpec((1, c, MIX), lambda bi, ti: (bi, ti, 0))
    state = pltpu.VMEM((HEADS, HEAD_DIM, HEAD_DIM), F32)
    return pl.pallas_call(
        functools.partial(_hgrn_ret_kernel, layer),
        grid=(bsz, length // c),
        in_specs=[col(0), col(1), col(2), col(3),
                  pl.BlockSpec(lb.shape, lambda bi, ti: (0, 0)),
                  pl.BlockSpec((1, MIX), lambda bi, ti: (0, 0)),
                  col(5), col(6), col(7), col(8), tab, tab],
        out_specs=[out, out],
        out_shape=[jax.ShapeDtypeStruct((bsz, length, MIX), BF16)] * 2,
        scratch_shapes=[state, pltpu.VMEM((c, c), jnp.int32), state],
        compiler_params=_params(("parallel", "arbitrary")),
        name="hgrn2_retention",
    )(proj3, proj3, proj3, proj3, lb, ng, proj3, proj3, proj3, proj3, cos_t, sin_t)


def _memkv_kernel(m_ref, g_ref, w_ref, k_ref, v_ref):
    kv = _dot(_rms(m_ref[0], g_ref[...]).astype(BF16), w_ref[...])
    k_ref[0] = kv[:, :D_MODEL].astype(BF16)
    v_ref[0] = kv[:, D_MODEL:].astype(BF16)


def _memkv(mem, gain, wkv_bf16):
    bsz, m, d = mem.shape
    spec_o = pl.BlockSpec((1, m, d), lambda bi: (bi, 0, 0))
    return pl.pallas_call(
        _memkv_kernel,
        grid=(bsz,),
        in_specs=[pl.BlockSpec((1, m, d), lambda bi: (bi, 0, 0)),
                  pl.BlockSpec((1, d), lambda bi: (0, 0)),
                  pl.BlockSpec((d, 2 * d), lambda bi: (0, 0))],
        out_specs=[spec_o, spec_o],
        out_shape=[jax.ShapeDtypeStruct((bsz, m, d), BF16)] * 2,
        compiler_params=_params(("parallel",)),
        name="mem_kv",
    )(mem, gain, wkv_bf16)


def _mix_kernel(x_ref, ga_ref, gb_ref, bg_ref, hg_ref, s5_ref, rt_ref, wb_ref, wout_ref,
                nc_ref, k_ref, v_ref, wq_ref, wo_ref, o_ref):
    d = D_MODEL
    logits = jnp.concatenate([ga_ref[...], gb_ref[...]], axis=1) + bg_ref[...]
    gates = _sigmoid(logits)
    merged = (gates[:, 0:d] * _dot(hg_ref[...], wb_ref[0:MIX, :])
              + gates[:, d:2 * d] * _dot(s5_ref[...], wb_ref[MIX:MIX + S5_WIDTH, :])
              + gates[:, 2 * d:3 * d] * _dot(rt_ref[...], wb_ref[MIX + S5_WIDTH:, :]))
    x1 = x_ref[...] + _dot(merged.astype(BF16), wout_ref[...])

    hc = _rms(x1, nc_ref[...]).astype(BF16)
    q = _dot(hc, wq_ref[...])
    kk = k_ref[0]
    vv = v_ref[0]
    outs = []
    for hh in range(XA_HEADS):
        sl = slice(XA_DH * hh, XA_DH * (hh + 1))
        s = _dot_nt(q[:, sl].astype(BF16), kk[:, sl]) * XA_DH ** -0.5
        p = jnp.exp(s - jnp.max(s, axis=-1, keepdims=True))
        p = p / jnp.sum(p, axis=-1, keepdims=True)
        outs.append(_dot(p.astype(BF16), vv[:, sl]))
    attn = jnp.concatenate(outs, axis=1)
    o_ref[...] = x1 + _dot(attn.astype(BF16), wo_ref[...])


def _mix(x2d, proj, bgate, o_hg, o_s5, o_rt, wb, wout, ncross, kmem, vmem, wq, wo, length, tm=512):
    m, d = x2d.shape
    per_batch = length // tm
    tok = lambda w: pl.BlockSpec((tm, w), lambda i: (i, 0))
    const = lambda shape: pl.BlockSpec(shape, lambda i: tuple(0 for _ in shape))
    gw = 1536
    mem_spec = pl.BlockSpec((1,) + kmem.shape[1:], lambda i: (i // per_batch, 0, 0))
    return pl.pallas_call(
        _mix_kernel,
        grid=(m // tm,),
        in_specs=[tok(d),
                  pl.BlockSpec((tm, gw), lambda i: (i, 3)), pl.BlockSpec((tm, gw), lambda i: (i, 4)),
                  const((1, 3 * d)), tok(MIX), tok(S5_WIDTH), tok(MIX),
                  const(wb.shape), const((d, d)), const((1, d)), mem_spec, mem_spec,
                  const((d, d)), const((d, d))],
        out_specs=tok(d),
        out_shape=jax.ShapeDtypeStruct((m, d), F32),
        compiler_params=_params(("parallel",)),
        name="merge_xattn",
    )(x2d, proj, proj, bgate, o_hg, o_s5, o_rt, wb, wout, ncross, kmem, vmem, wq, wo)


def _sorting_network(n):
    pairs = []
    p = 1
    while p < n:
        k = p
        while k >= 1:
            for j in range(k % p, n - k, 2 * k):
                for i in range(min(k, n - j - k)):
                    if (i + j) // (2 * p) == (i + j + k) // (2 * p):
                        pairs.append((i + j, i + j + k))
            k //= 2
        p *= 2
    return pairs


def _top_values(s):
    n_keys, t = s.shape
    slabs = [s[8 * i:8 * (i + 1), :] for i in range(n_keys // 8)]
    for lo, hi in _sorting_network(len(slabs)):
        slabs[lo], slabs[hi] = jnp.maximum(slabs[lo], slabs[hi]), jnp.minimum(slabs[lo], slabs[hi])
    row16 = lax.broadcasted_iota(jnp.int32, (PEER_TOPK, t), 0)
    vals = jnp.zeros((PEER_TOPK, t), F32)
    for r in range(PEER_TOPK):
        m = jnp.max(slabs[0], axis=0, keepdims=True)
        hit = slabs[0] == m
        for i in range(PEER_TOPK - 1 - r):
            slabs[i] = jnp.where(hit, slabs[i + 1], slabs[i])
        vals = jnp.where(row16 == r, m, vals)
    return vals


def _route_kernel(x_ref, g_ref, wqt_ref, keys_ref, h_ref, rank2_ref, e2_ref, n1_ref, e1_ref):
    t = x_ref.shape[0]
    k = PEER_TOPK
    hb = _rms(x_ref[...], g_ref[...]).T.astype(BF16)
    h_ref[...] = hb
    qt = _dot(wqt_ref[...], hb).astype(BF16)
    row16 = lax.broadcasted_iota(jnp.int32, (k, t), 0)
    row8 = lax.broadcasted_iota(jnp.int32, (8, t), 0)
    zeros8 = jnp.zeros((8, t), F32)
    for hd in range(PEER_HEADS):
        base = hd * 2 * PEER_DH
        s1 = _dot(keys_ref[2 * hd], qt[base:base + PEER_DH])
        s2 = _dot(keys_ref[2 * hd + 1], qt[base + PEER_DH:base + 2 * PEER_DH])
        a = _top_values(s1)
        b = _top_values(s2)
        rank2 = jnp.full(s2.shape, UNRANKED, F32)
        for jr in range(k, 0, -1):
            rank2 = jnp.where(s2 >= b[jr - 1:jr], float(jr), rank2)
        pieces = [a + b[0:1]]
        for j in range(2, 9):
            pieces.append(jnp.where(row8 < k // j, a[0:8] + b[j - 1:j], NEG_INF))
        pieces.append(a[0:1] + b[8:16])
        cand = jnp.concatenate(pieces, axis=0)
        top = a[0:1] + b[0:1]
        work = cand
        thr = top
        for r in range(k):
            thr = jnp.max(work, axis=0, keepdims=True)
            work = jnp.where(work == thr, NEG_INF, work)
        sel = cand >= thr
        zsum = jnp.sum(jnp.where(sel, jnp.exp(cand - top), 0.0), axis=0, keepdims=True)
        picked = sel.astype(F32)
        low = picked[16:24]
        for j in range(3, 9):
            low = low + picked[8 * j:8 * j + 8]
        cnt_tail = jnp.sum(picked[72:80], axis=0, keepdims=True)
        n_of_i = picked[0:16] + jnp.concatenate([low, zeros8], axis=0) + jnp.where(row16 == 0, cnt_tail, 0.0)
        n1 = jnp.zeros(s1.shape, F32)
        for i in range(k):
            n1 = jnp.where(s1 == a[i:i + 1], n_of_i[i:i + 1], n1)
        rows = slice(hd * PEER_NKEYS, (hd + 1) * PEER_NKEYS)
        rank2_ref[rows, :] = rank2.astype(BF16)
        e2_ref[rows, :] = (0.5 * jnp.exp(s2 - b[0:1]) / zsum).astype(BF16)
        n1_ref[rows, :] = n1
        e1_ref[rows, :] = jnp.exp(s1 - a[0:1])


def _route(x2d, gain, wqt, keys2, tt=256):
    m, d = x2d.shape
    rows = PEER_HEADS * PEER_NKEYS
    tab = pl.BlockSpec((rows, tt), lambda i: (0, i))
    return pl.pallas_call(
        _route_kernel,
        grid=(m // tt,),
        in_specs=[pl.BlockSpec((tt, d), lambda i: (i, 0)),
                  pl.BlockSpec((1, d), lambda i: (0, 0)),
                  pl.BlockSpec(wqt.shape, lambda i: (0, 0)),
                  pl.BlockSpec(keys2.shape, lambda i: (0, 0, 0))],
        out_specs=[pl.BlockSpec((d, tt), lambda i: (0, i)), tab, tab, tab, tab],
        out_shape=[jax.ShapeDtypeStruct((d, m), BF16)] + [jax.ShapeDtypeStruct((rows, m), BF16)] * 2
        + [jax.ShapeDtypeStruct((rows, m), F32)] * 2,
        compiler_params=_params(("parallel",)),
        name="peer_route",
    )(x2d, gain, wqt, keys2)


def _peer_kernel(final, x_ref, h_ref, rank2_ref, e2_ref, n1_ref, e1_ref, u_ref, vt_ref, gf_ref,
                 o_ref, acc_scr):
    j = pl.program_id(1)
    eb = u_ref.shape[0]
    nk = PEER_NKEYS

    @pl.when(j == 0)
    def _():
        acc_scr[...] = jnp.zeros_like(acc_scr)

    tt = h_ref.shape[1]
    half = tt // 2
    i1_base = pl.multiple_of(j * (eb // nk), eb // nk)
    sizes = [s for s in PEER_CHUNK_BLOCKS]
    assert sum(sizes) * nk == eb
    starts = [sum(sizes[:c]) for c in range(len(sizes))]
    rows_of = [slice(st * nk, (st + sz) * nk) for st, sz in zip(starts, sizes)]

    def score(ck, th):
        return jnp.dot(u_ref[rows_of[ck], :], h_ref[:, th * half:(th + 1) * half],
                       preferred_element_type=F32).astype(BF16)

    def key_row(ref, hd, il, cols):
        r = ref[pl.ds(hd * nk + i1_base + il, 1), :][:, cols]
        return jnp.tile(jnp.broadcast_to(r, (BF16_ROWS, half)).astype(BF16), (nk // BF16_ROWS, 1))

    def act_tile(z_half, ck, sub, th):
        il = starts[ck] + sub
        cols = slice(th * half, (th + 1) * half)
        gate = None
        for hd in range(PEER_HEADS):
            rows = slice(hd * nk, (hd + 1) * nk)
            sel = jnp.where(rank2_ref[rows, cols] <= key_row(n1_ref, hd, il, cols), e2_ref[rows, cols], 0.0)
            term = sel * key_row(e1_ref, hd, il, cols)
            gate = term if gate is None else gate + term
        return _gelu2(z_half[sub * nk:(sub + 1) * nk, :]) * gate

    def apply(ck, th, act):
        cols = slice(th * half, (th + 1) * half)
        acc_scr[:, cols] += _dot(vt_ref[:, rows_of[ck]], act)

    z = [score(0, 0), score(0, 1)]
    prev = None
    for ck, n_sub in enumerate(sizes):
        pieces = []
        z_next = [None, None]
        if ck + 1 < len(sizes):
            pieces += [("score", 0), ("score", 1)]
        if prev is not None:
            pieces += [("apply", 0), ("apply", 1)]
        acts = ([], [])
        for sub in range(n_sub):
            for th in range(2):
                acts[th].append(act_tile(z[th], ck, sub, th))
            for kind, th in pieces[sub * len(pieces) // n_sub:(sub + 1) * len(pieces) // n_sub]:
                if kind == "score":
                    z_next[th] = score(ck + 1, th)
                else:
                    apply(ck - 1, th, prev[th])
        prev = [jnp.concatenate(acts[0], axis=0), jnp.concatenate(acts[1], axis=0)]
        z = z_next
    apply(len(sizes) - 1, 0, prev[0])
    apply(len(sizes) - 1, 1, prev[1])

    @pl.when(j == pl.num_programs(1) - 1)
    def _():
        y = x_ref[...] + acc_scr[...].T
        if final:
            y = _rms(y, gf_ref[...])
        o_ref[...] = y


def _peer(x2d, h, rank2, e2, n1, e1, u_layers, vt_layers, layer, gfinal, final, tt=512, eb=2048):
    m, d = x2d.shape
    ne = u_layers.shape[1]
    rows = PEER_HEADS * PEER_NKEYS
    tab = pl.BlockSpec((rows, tt), lambda i, j: (0, i))
    return pl.pallas_call(
        functools.partial(_peer_kernel, final),
        grid=(m // tt, ne // eb),
        in_specs=[pl.BlockSpec((tt, d), lambda i, j: (i, 0)),
                  pl.BlockSpec((d, tt), lambda i, j: (0, i)),
                  tab, tab, tab, tab,
                  pl.BlockSpec((None, eb, d), lambda i, j: (layer, j, 0)),
                  pl.BlockSpec((None, d, eb), lambda i, j: (layer, 0, j)),
                  pl.BlockSpec((1, d), lambda i, j: (0, 0))],
        out_specs=pl.BlockSpec((tt, d), lambda i, j: (i, 0)),
        out_shape=jax.ShapeDtypeStruct((m, d), F32),
        scratch_shapes=[pltpu.VMEM((d, tt), F32)],
        compiler_params=_params(("parallel", "arbitrary")),
        name="peer_experts",
    )(x2d, h, rank2, e2, n1, e1, u_layers, vt_layers, gfinal)


def kernel(x, mem, positions, norm_mix, w_in, b_gate, hgrn_lb, hgrn_norm, s5_lambda_re, s5_lambda_im,
           s5_log_step, s5_b_re, s5_b_im, s5_c_re, s5_c_im, s5_d, s5_w_glu, s5_b_glu, w_branch, w_out,
           norm_cross, norm_mem, xa_wq, xa_wkv, xa_wo, norm_ffn, peer_wq, peer_keys, peer_u, peer_v,
           norm_final):
    bsz, length, d = x.shape
    depth = w_in.shape[0]
    row = lambda a: a.astype(F32).reshape(1, -1)
    x2d = x.astype(F32).reshape(bsz * length, d)
    cos_t, sin_t = _rope_tables(positions)
    lb = hgrn_lb.astype(F32)
    w_in_b = w_in.astype(BF16)
    peer_u_b = peer_u.astype(BF16)
    peer_vt_b = peer_v.astype(BF16).transpose(0, 2, 1)
    for l in range(depth):
        proj = _inproj(x2d, row(norm_mix[l]), w_in_b, l)
        proj3 = proj.reshape(bsz, length, IN_COLS)
        o_hg, o_rt = _hgrn_retention(proj3, lb, row(hgrn_norm[l]), l, cos_t, sin_t)
        bb_re, bb_im, tab = _s5_prep(s5_lambda_re[l], s5_lambda_im[l], s5_log_step[l], s5_b_re[l], s5_b_im[l])
        bbd = jnp.concatenate([_block_diag_in(bb_re), _block_diag_in(bb_im)], axis=2).astype(BF16)
        o_s5 = _s5(proj3, bbd, tab, _block_diag_out(s5_c_re[l]).astype(BF16),
                   _block_diag_out(s5_c_im[l]).astype(BF16), row(s5_d[l]),
                   s5_w_glu[l].astype(BF16), row(s5_b_glu[l]))
        kmem, vmem = _memkv(mem.astype(F32), row(norm_mem[l]), xa_wkv[l].astype(BF16))
        x2d = _mix(x2d, proj, row(b_gate[l]), o_hg.reshape(-1, MIX), o_s5.reshape(-1, S5_WIDTH),
                   o_rt.reshape(-1, MIX), w_branch[l].astype(BF16), w_out[l].astype(BF16),
                   row(norm_cross[l]), kmem, vmem, xa_wq[l].astype(BF16), xa_wo[l].astype(BF16), length)
        keys2 = peer_keys[l].astype(BF16).reshape(PEER_HEADS * 2, PEER_NKEYS, PEER_DH)
        h, rank2, e2, n1, e1 = _route(x2d, row(norm_ffn[l]), peer_wq[l].astype(BF16).T, keys2)
        x2d = _peer(x2d, h, rank2, e2, n1, e1, peer_u_b, peer_vt_b, l, row(norm_final), final=(l == depth - 1))
    return x2d.reshape(bsz, length, d)
```

```python
import functools
import math

import jax
import jax.numpy as jnp
from jax import lax
from jax.experimental import pallas as pl
from jax.experimental.pallas import tpu as pltpu

F32 = jnp.float32
BF16 = jnp.bfloat16

D_MODEL = 1024
HEADS = 4
HEAD_DIM = 128
MIX = HEADS * HEAD_DIM
HG_F_MIN = 1e-6
S5_GROUPS = 32
S5_GROUP = 16
S5_STATE = 64
S5_WIDTH = S5_GROUPS * S5_GROUP
S5_LANES = S5_GROUPS * S5_STATE
S5_BLOCKS = 4
RET_CHUNK = 256
ROPE_BASE = 10000.0
IN_COLS = 4 * MIX + S5_WIDTH + 4 * MIX + 3 * D_MODEL
XA_HEADS = 4
XA_DH = D_MODEL // XA_HEADS
PEER_HEADS = 8
PEER_NKEYS = 128
PEER_DH = 128
PEER_TOPK = 16
EPS = 1e-6
NEG_INF = float("-inf")
UNRANKED = 99.0
BF16_ROWS = 16
PEER_CHUNK_BLOCKS = (4, 4, 4, 4)

VMEM_LIMIT = 56 * 1024 * 1024

NT_DIMS = (((1,), (1,)), ((), ()))
TN_DIMS = (((0,), (0,)), ((), ()))


def _params(sem):
    return pltpu.CompilerParams(dimension_semantics=sem, vmem_limit_bytes=VMEM_LIMIT)


def _rms(x, gain):
    return x * lax.rsqrt(jnp.mean(x * x, axis=-1, keepdims=True) + EPS) * gain


def _gelu(x):
    return 0.5 * x * (1.0 + jnp.tanh(math.sqrt(2.0 / math.pi) * (x + 0.044715 * (x * x * x))))


def _gelu2(x):
    c = math.sqrt(2.0 / math.pi)
    return x + x * jnp.tanh(x * (c + (c * 0.044715) * (x * x)))


def _sigmoid(x):
    return 1.0 / (1.0 + jnp.exp(-x))


def _silu(x):
    return x * _sigmoid(x)


def _dot(a, b):
    return jnp.dot(a, b, preferred_element_type=F32)


def _dot_nt(a, b):
    return lax.dot_general(a, b, NT_DIMS, preferred_element_type=F32)


def _dot_tn(a, b):
    return lax.dot_general(a, b, TN_DIMS, preferred_element_type=F32)


def _inproj_kernel(x_ref, g_ref, w_ref, o_ref, h_scr):
    @pl.when(pl.program_id(1) == 0)
    def _():
        h_scr[...] = _rms(x_ref[...], g_ref[...]).astype(BF16)

    o_ref[...] = _dot(h_scr[...], w_ref[...])


def _inproj(x2d, gain, w_layers, layer, tm=1024, tn=2560):
    m, d = x2d.shape
    n = w_layers.shape[2]
    return pl.pallas_call(
        _inproj_kernel,
        grid=(m // tm, n // tn),
        in_specs=[
            pl.BlockSpec((tm, d), lambda i, j: (i, 0)),
            pl.BlockSpec((1, d), lambda i, j: (0, 0)),
            pl.BlockSpec((None, d, tn), lambda i, j: (layer, 0, j)),
        ],
        out_specs=pl.BlockSpec((tm, tn), lambda i, j: (i, j)),
        out_shape=jax.ShapeDtypeStruct((m, n), F32),
        scratch_shapes=[pltpu.VMEM((tm, d), BF16)],
        compiler_params=_params(("parallel", "arbitrary")),
        name="inproj",
    )(x2d, gain, w_layers)


def _hgrn_kernel(layer, q_ref, f_ref, i_ref, g_ref, lb_ref, ng_ref, o_ref, st_scr, lvl_scr):
    tc = q_ref.shape[1]
    n_lvl = tc.bit_length() - 1

    @pl.when(pl.program_id(1) == 0)
    def _():
        st_scr[...] = jnp.zeros_like(st_scr)
        ti = lax.broadcasted_iota(jnp.int32, (tc, tc), 0)
        si = lax.broadcasted_iota(jnp.int32, (tc, tc), 1)
        top_bit = (lax.bitcast_convert_type((ti ^ si).astype(F32), jnp.int32) >> 23) - 127
        lvl_scr[...] = jnp.where(si < ti, top_bit, jnp.where(si == ti, n_lvl, -1))

    lbs = lb_ref[...]
    e = jnp.exp(lbs - jnp.max(lbs, axis=0, keepdims=True))
    sm = e / jnp.sum(e, axis=0, keepdims=True)
    lower = jnp.sum(sm[0:layer + 1], axis=0, keepdims=True) - sm[0:1]

    z = f_ref[0]
    fv = lower + (1.0 - lower) * _sigmoid(z)
    logf = jnp.log(jnp.clip(fv, HG_F_MIN, 1.0))
    kk = (1.0 - lower) * _sigmoid(-z)
    qs = q_ref[0] * HEAD_DIM ** -0.5
    vb = i_ref[0].astype(BF16)
    heads = [slice(HEAD_DIM * hh, HEAD_DIM * (hh + 1)) for hh in range(HEADS)]

    lvl = lvl_scr[...]
    qb = qs.astype(BF16)
    kb = kk.astype(BF16)
    att = [jnp.where(lvl == n_lvl, _dot_nt(qb[:, sl], kb[:, sl]), 0.0) for sl in heads]

    rowi = lax.broadcasted_iota(jnp.int32, (tc, MIX), 0)
    pre = logf
    post = jnp.zeros_like(logf)
    for level in range(n_lvl):
        h = 1 << level
        upper = (rowi & h) != 0
        x = (jnp.where(upper, qs, kk) * jnp.exp(jnp.where(upper, pre, post))).astype(BF16)
        here = lvl == level
        att = [jnp.where(here, _dot_nt(x[:, sl], x[:, sl]), a) for sl, a in zip(heads, att)]
        total = pre + post
        pre = pre + jnp.where(upper, pltpu.roll(total, h, 0), 0.0)
        post = post + jnp.where(upper, 0.0, pltpu.roll(total, tc - h, 0))

    qd = (qs * jnp.exp(pre)).astype(BF16)
    kd = (kk * jnp.exp(post)).astype(BF16)
    decay = jnp.exp(pre[tc - 1:tc, :])
    gate = g_ref[0]
    ng = ng_ref[...]
    outs = []
    for hh, sl in enumerate(heads):
        st = st_scr[hh]
        oh = _dot(att[hh].astype(BF16), vb[:, sl]) + _dot_nt(qd[:, sl], st.astype(BF16))
        st_scr[hh] = st * decay[:, sl] + _dot_tn(vb[:, sl], kd[:, sl])
        outs.append(oh * lax.rsqrt(jnp.mean(oh * oh, axis=-1, keepdims=True) + EPS) * ng[:, sl])
    o_ref[0] = (jnp.concatenate(outs, axis=1) * _silu(gate)).astype(o_ref.dtype)


def _s5_prep_kernel(lr_ref, li_ref, ls_ref, bre_ref, bim_ref, bbre_ref, bbim_ref, tab_ref):
    lr = jnp.minimum(lr_ref[...], -1e-4)
    li = li_ref[...]
    dt = jnp.exp(ls_ref[...])
    mag = jnp.exp(lr * dt)
    a_re = mag * jnp.cos(li * dt)
    a_im = mag * jnp.sin(li * dt)
    den = lr * lr + li * li
    z_re = ((a_re - 1.0) * lr + a_im * li) / den
    z_im = (a_im * lr - (a_re - 1.0) * li) / den
    bre = bre_ref[...]
    bim = bim_ref[...]
    bbre_ref[...] = z_re * bre - z_im * bim
    bbim_ref[...] = z_re * bim + z_im * bre
    row = lax.broadcasted_iota(jnp.int32, (8, a_re.shape[1]), 0)
    p_re = jnp.broadcast_to(a_re, row.shape)
    p_im = jnp.broadcast_to(a_im, row.shape)
    c_re, c_im = a_re, a_im
    powers = [(a_re, a_im)]
    for r in range(1, 8):
        c_re, c_im = c_re * a_re - c_im * a_im, c_re * a_im + c_im * a_re
        powers.append((c_re, c_im))
        p_re = jnp.where(row == r, c_re, p_re)
        p_im = jnp.where(row == r, c_im, p_im)
    for slot, d in enumerate((1, 2, 4)):
        tab_ref[2 * slot] = jnp.where(row >= d, powers[d - 1][0], 0.0)
        tab_ref[2 * slot + 1] = jnp.where(row >= d, powers[d - 1][1], 0.0)
    tab_ref[6] = p_re
    tab_ref[7] = p_im


def _s5_prep(lam_re, lam_im, log_step, b_re, b_im):
    lanes = S5_LANES
    row = lambda a: a.astype(F32).reshape(1, lanes)
    ls = jnp.repeat(log_step.astype(F32), S5_STATE).reshape(1, lanes)
    bt = lambda a: a.astype(F32).reshape(lanes, S5_GROUP).T
    full = lambda shape: pl.BlockSpec(shape, lambda: tuple(0 for _ in shape))
    return pl.pallas_call(
        _s5_prep_kernel,
        in_specs=[full((1, lanes))] * 3 + [full((S5_GROUP, lanes))] * 2,
        out_specs=[full((S5_GROUP, lanes))] * 2 + [full((8, 8, lanes))],
        out_shape=[jax.ShapeDtypeStruct((S5_GROUP, lanes), F32)] * 2
        + [jax.ShapeDtypeStruct((8, 8, lanes), F32)],
        name="s5_prep",
    )(row(lam_re), row(lam_im), ls, bt(b_re), bt(b_im))


def _s5_kernel(u_ref, bbd_ref, tab_ref, cre_ref, cim_ref, d_ref, wg_ref, bg_ref, o_ref,
               xr_scr, xi_scr, car_scr):
    tt = u_ref.shape[1]
    lanes = S5_LANES
    lw = 512

    @pl.when(pl.program_id(1) == 0)
    def _():
        car_scr[...] = jnp.zeros_like(car_scr)

    u = u_ref[0]
    ub = u.astype(BF16)
    n_blk = bbd_ref.shape[0]
    cw = S5_WIDTH // n_blk
    bl = lanes // n_blk
    for jb in range(n_blk):
        bu = _dot(ub[:, cw * jb:cw * (jb + 1)], bbd_ref[jb])
        xr_scr[:, bl * jb:bl * (jb + 1)] = bu[:, :bl]
        xi_scr[:, bl * jb:bl * (jb + 1)] = bu[:, bl:]

    for lc in range(lanes // lw):
        sl = slice(lw * lc, lw * (lc + 1))
        steps = [(d, tab_ref[2 * s][:, sl], tab_ref[2 * s + 1][:, sl]) for s, d in enumerate((1, 2, 4))]
        p_re = tab_ref[6][:, sl]
        p_im = tab_ref[7][:, sl]

        def body(k, carry, sl=sl, steps=steps, p_re=p_re, p_im=p_im):
            c_re, c_im = carry
            r0 = pl.multiple_of(k * 8, 8)
            xr = xr_scr[pl.ds(r0, 8), sl]
            xi = xi_scr[pl.ds(r0, 8), sl]
            for d, a_re, a_im in steps:
                sr = pltpu.roll(xr, d, 0)
                si = pltpu.roll(xi, d, 0)
                xr, xi = xr + a_re * sr - a_im * si, xi + a_re * si + a_im * sr
            xr, xi = xr + p_re * c_re - p_im * c_im, xi + p_re * c_im + p_im * c_re
            xr_scr[pl.ds(r0, 8), sl] = xr
            xi_scr[pl.ds(r0, 8), sl] = xi
            return xr[7:8, :], xi[7:8, :]

        c_re, c_im = lax.fori_loop(0, tt // 8, body, (car_scr[0:1, sl], car_scr[1:2, sl]), unroll=True)
        car_scr[0:1, sl] = c_re
        car_scr[1:2, sl] = c_im

    ys = []
    for jb in range(n_blk):
        cols = slice(bl * jb, bl * (jb + 1))
        ys.append(_dot(xr_scr[:, cols].astype(BF16), cre_ref[jb]) - _dot(xi_scr[:, cols].astype(BF16), cim_ref[jb]))
    y = jnp.concatenate(ys, axis=1) + d_ref[...] * u
    y = _gelu(y)
    o_ref[0] = (y * _sigmoid(_dot(y.astype(BF16), wg_ref[...]) + bg_ref[...])).astype(o_ref.dtype)


def _s5(proj3, bbd, tab, cre, cim, dskip, wg, bg, tt=256):
    bsz, length, _ = proj3.shape
    lanes = S5_LANES
    const = lambda shape: pl.BlockSpec(shape, lambda bi, ti: tuple(0 for _ in shape))
    return pl.pallas_call(
        _s5_kernel,
        grid=(bsz, length // tt),
        in_specs=[pl.BlockSpec((1, tt, S5_WIDTH), lambda bi, ti: (bi, ti, 4)),
                  const(bbd.shape), const((8, 8, lanes)), const(cre.shape), const(cim.shape),
                  const((1, S5_WIDTH)), const((S5_WIDTH, S5_WIDTH)), const((1, S5_WIDTH))],
        out_specs=pl.BlockSpec((1, tt, S5_WIDTH), lambda bi, ti: (bi, ti, 0)),
        out_shape=jax.ShapeDtypeStruct((bsz, length, S5_WIDTH), BF16),
        scratch_shapes=[pltpu.VMEM((tt, lanes), F32), pltpu.VMEM((tt, lanes), F32),
                        pltpu.VMEM((8, lanes), F32)],
        compiler_params=_params(("parallel", "arbitrary")),
        name="s5",
    )(proj3, bbd, tab, cre, cim, dskip, wg, bg)


def _block_diag_in(bb_t):
    g, c, p = S5_GROUPS, S5_GROUP, S5_STATE
    gb = g // S5_BLOCKS
    per_group = bb_t.reshape(c, S5_BLOCKS, gb, p).transpose(1, 2, 0, 3)
    eye = jnp.eye(gb, dtype=F32)
    full = eye[None, :, None, :, None] * per_group[:, :, :, None, :]
    return full.reshape(S5_BLOCKS, gb * c, gb * p)


def _block_diag_out(cmat):
    g, c, p = S5_GROUPS, S5_GROUP, S5_STATE
    gb = g // S5_BLOCKS
    per_group = cmat.astype(F32).reshape(S5_BLOCKS, gb, c, p).transpose(0, 1, 3, 2)
    eye = jnp.eye(gb, dtype=F32)
    full = eye[None, :, None, :, None] * per_group[:, :, :, None, :]
    return full.reshape(S5_BLOCKS, gb * p, gb * c)


def _rope_kernel(pos_ref, cos_ref, sin_ref):
    pos = pos_ref[0].astype(F32)
    half = HEAD_DIM // 2
    lane = lax.broadcasted_iota(jnp.int32, (1, HEAD_DIM), 1)
    j = (lane & (half - 1)).astype(F32)
    inv = jnp.exp(j * (-math.log(ROPE_BASE) / half))
    ang = pos * inv
    cos_ref[0] = jnp.cos(ang)
    sin_ref[0] = jnp.where(lane < half, -1.0, 1.0) * jnp.sin(ang)


def _rope_tables(positions, tr=1024):
    bsz, length = positions.shape
    tr = min(tr, length)
    spec_o = pl.BlockSpec((1, tr, HEAD_DIM), lambda bi, ti: (bi, ti, 0))
    return pl.pallas_call(
        _rope_kernel,
        grid=(bsz, length // tr),
        in_specs=[pl.BlockSpec((1, tr, 1), lambda bi, ti: (bi, ti, 0))],
        out_specs=[spec_o, spec_o],
        out_shape=[jax.ShapeDtypeStruct((bsz, length, HEAD_DIM), F32)] * 2,
        compiler_params=_params(("parallel", "parallel")),
        name="rope_tables",
    )(positions.reshape(bsz, length, 1))


def _ret_kernel(q_ref, k_ref, v_ref, g_ref, cos_ref, sin_ref, o_ref, s_scr):
    c = RET_CHUNK

    @pl.when(pl.program_id(1) == 0)
    def _():
        s_scr[...] = jnp.zeros_like(s_scr)

    cosf = cos_ref[0]
    sinf = sin_ref[0]
    rel = (lax.broadcasted_iota(jnp.int32, (c, c), 0) - lax.broadcasted_iota(jnp.int32, (c, c), 1)).astype(F32)
    idx = lax.broadcasted_iota(jnp.int32, (c, 1), 0).astype(F32)
    qa, ka, va, ga = q_ref[0], k_ref[0], v_ref[0], g_ref[0]
    outs = []
    for hh in range(HEADS):
        sl = slice(HEAD_DIM * hh, HEAD_DIM * (hh + 1))
        lg = math.log1p(-2.0 ** (-5.0 - hh))
        q = qa[:, sl]
        k = ka[:, sl]
        vb = va[:, sl].astype(BF16)
        qh = q * cosf + pltpu.roll(q, HEAD_DIM // 2, 1) * sinf
        kh = (k * cosf + pltpu.roll(k, HEAD_DIM // 2, 1) * sinf) * HEAD_DIM ** -0.5
        dmat = jnp.where(rel >= 0.0, jnp.exp(lg * jnp.maximum(rel, 0.0)), 0.0)
        scores = _dot_nt(qh.astype(BF16), kh.astype(BF16)) * dmat
        o = _dot(scores.astype(BF16), vb)
        q_dec = qh * jnp.exp(lg * (idx + 1.0))
        k_dec = kh * jnp.exp(lg * (c - 1.0 - idx))
        st = s_scr[hh]
        o = o + _dot(q_dec.astype(BF16), st.astype(BF16))
        s_scr[hh] = math.exp(lg * c) * st + _dot_tn(k_dec.astype(BF16), vb)
        mu = jnp.mean(o, axis=-1, keepdims=True)
        var = jnp.mean((o - mu) ** 2, axis=-1, keepdims=True)
        outs.append((o - mu) * lax.rsqrt(var + EPS))
    o_ref[0] = (jnp.concatenate(outs, axis=1) * _silu(ga)).astype(o_ref.dtype)


def _hgrn_ret_kernel(layer, q_ref, f_ref, i_ref, g_ref, lb_ref, ng_ref, rq_ref, rk_ref, rv_ref, rg_ref,
                     cos_ref, sin_ref, ohg_ref, ort_ref, st_scr, lvl_scr, s_scr):
    _hgrn_kernel(layer, q_ref, f_ref, i_ref, g_ref, lb_ref, ng_ref, ohg_ref, st_scr, lvl_scr)
    _ret_kernel(rq_ref, rk_ref, rv_ref, rg_ref, cos_ref, sin_ref, ort_ref, s_scr)


def _hgrn_retention(proj3, lb, ng, layer, cos_t, sin_t):
    bsz, length, _ = proj3.shape
    c = RET_CHUNK
    col = lambda k: pl.BlockSpec((1, c, MIX), lambda bi, ti, k=k: (bi, ti, k))
    tab = pl.BlockSpec((1, c, HEAD_DIM), lambda bi, ti: (bi, ti, 0))
    out = pl.BlockSpec((1, c, MIX), lambda bi, ti: (bi, ti, 0))
    state = pltpu.VMEM((HEADS, HEAD_DIM, HEAD_DIM), F32)
    return pl.pallas_call(
        functools.partial(_hgrn_ret_kernel, layer),
        grid=(bsz, length // c),
        in_specs=[col(0), col(1), col(2), col(3),
                  pl.BlockSpec(lb.shape, lambda bi, ti: (0, 0)),
                  pl.BlockSpec((1, MIX), lambda bi, ti: (0, 0)),
                  col(5), col(6), col(7), col(8), tab, tab],
        out_specs=[out, out],
        out_shape=[jax.ShapeDtypeStruct((bsz, length, MIX), BF16)] * 2,
        scratch_shapes=[state, pltpu.VMEM((c, c), jnp.int32), state],
        compiler_params=_params(("parallel", "arbitrary")),
        name="hgrn2_retention",
    )(proj3, proj3, proj3, proj3, lb, ng, proj3, proj3, proj3, proj3, cos_t, sin_t)


def _memkv_kernel(m_ref, g_ref, w_ref, k_ref, v_ref):
    kv = _dot(_rms(m_ref[0], g_ref[...]).astype(BF16), w_ref[...])
    k_ref[0] = kv[:, :D_MODEL].astype(BF16)
    v_ref[0] = kv[:, D_MODEL:].astype(BF16)


def _memkv(mem, gain, wkv_bf16):
    bsz, m, d = mem.shape
    spec_o = pl.BlockSpec((1, m, d), lambda bi: (bi, 0, 0))
    return pl.pallas_call(
        _memkv_kernel,
        grid=(bsz,),
        in_specs=[pl.BlockSpec((1, m, d), lambda bi: (bi, 0, 0)),
                  pl.BlockSpec((1, d), lambda bi: (0, 0)),
                  pl.BlockSpec((d, 2 * d), lambda bi: (0, 0))],
        out_specs=[spec_o, spec_o],
        out_shape=[jax.ShapeDtypeStruct((bsz, m, d), BF16)] * 2,
        compiler_params=_params(("parallel",)),
        name="mem_kv",
    )(mem, gain, wkv_bf16)


def _mix_kernel(x_ref, ga_ref, gb_ref, bg_ref, hg_ref, s5_ref, rt_ref, wb_ref, wout_ref,
                nc_ref, k_ref, v_ref, wq_ref, wo_ref, o_ref):
    d = D_MODEL
    logits = jnp.concatenate([ga_ref[...], gb_ref[...]], axis=1) + bg_ref[...]
    gates = _sigmoid(logits)
    merged = (gates[:, 0:d] * _dot(hg_ref[...], wb_ref[0:MIX, :])
              + gates[:, d:2 * d] * _dot(s5_ref[...], wb_ref[MIX:MIX + S5_WIDTH, :])
              + gates[:, 2 * d:3 * d] * _dot(rt_ref[...], wb_ref[MIX + S5_WIDTH:, :]))
    x1 = x_ref[...] + _dot(merged.astype(BF16), wout_ref[...])

    hc = _rms(x1, nc_ref[...]).astype(BF16)
    q = _dot(hc, wq_ref[...])
    kk = k_ref[0]
    vv = v_ref[0]
    outs = []
    for hh in range(XA_HEADS):
        sl = slice(XA_DH * hh, XA_DH * (hh + 1))
        s = _dot_nt(q[:, sl].astype(BF16), kk[:, sl]) * XA_DH ** -0.5
        p = jnp.exp(s - jnp.max(s, axis=-1, keepdims=True))
        p = p / jnp.sum(p, axis=-1, keepdims=True)
        outs.append(_dot(p.astype(BF16), vv[:, sl]))
    attn = jnp.concatenate(outs, axis=1)
    o_ref[...] = x1 + _dot(attn.astype(BF16), wo_ref[...])


def _mix(x2d, proj, bgate, o_hg, o_s5, o_rt, wb, wout, ncross, kmem, vmem, wq, wo, length, tm=512):
    m, d = x2d.shape
    per_batch = length // tm
    tok = lambda w: pl.BlockSpec((tm, w), lambda i: (i, 0))
    const = lambda shape: pl.BlockSpec(shape, lambda i: tuple(0 for _ in shape))
    gw = 1536
    mem_spec = pl.BlockSpec((1,) + kmem.shape[1:], lambda i: (i // per_batch, 0, 0))
    return pl.pallas_call(
        _mix_kernel,
        grid=(m // tm,),
        in_specs=[tok(d),
                  pl.BlockSpec((tm, gw), lambda i: (i, 3)), pl.BlockSpec((tm, gw), lambda i: (i, 4)),
                  const((1, 3 * d)), tok(MIX), tok(S5_WIDTH), tok(MIX),
                  const(wb.shape), const((d, d)), const((1, d)), mem_spec, mem_spec,
                  const((d, d)), const((d, d))],
        out_specs=tok(d),
        out_shape=jax.ShapeDtypeStruct((m, d), F32),
        compiler_params=_params(("parallel",)),
        name="merge_xattn",
    )(x2d, proj, proj, bgate, o_hg, o_s5, o_rt, wb, wout, ncross, kmem, vmem, wq, wo)


def _sorting_network(n):
    pairs = []
    p = 1
    while p < n:
        k = p
        while k >= 1:
            for j in range(k % p, n - k, 2 * k):
                for i in range(min(k, n - j - k)):
                    if (i + j) // (2 * p) == (i + j + k) // (2 * p):
                        pairs.append((i + j, i + j + k))
            k //= 2
        p *= 2
    return pairs


def _top_values(s):
    n_keys, t = s.shape
    slabs = [s[8 * i:8 * (i + 1), :] for i in range(n_keys // 8)]
    for lo, hi in _sorting_network(len(slabs)):
        slabs[lo], slabs[hi] = jnp.maximum(slabs[lo], slabs[hi]), jnp.minimum(slabs[lo], slabs[hi])
    row16 = lax.broadcasted_iota(jnp.int32, (PEER_TOPK, t), 0)
    vals = jnp.zeros((PEER_TOPK, t), F32)
    for r in range(PEER_TOPK):
        m = jnp.max(slabs[0], axis=0, keepdims=True)
        hit = slabs[0] == m
        for i in range(PEER_TOPK - 1 - r):
            slabs[i] = jnp.where(hit, slabs[i + 1], slabs[i])
        vals = jnp.where(row16 == r, m, vals)
    return vals


def _route_kernel(x_ref, g_ref, wqt_ref, keys_ref, h_ref, rank2_ref, e2_ref, n1_ref, e1_ref):
    t = x_ref.shape[0]
    k = PEER_TOPK
    hb = _rms(x_ref[...], g_ref[...]).T.astype(BF16)
    h_ref[...] = hb
    qt = _dot(wqt_ref[...], hb).astype(BF16)
    row16 = lax.broadcasted_iota(jnp.int32, (k, t), 0)
    row8 = lax.broadcasted_iota(jnp.int32, (8, t), 0)
    zeros8 = jnp.zeros((8, t), F32)
    for hd in range(PEER_HEADS):
        base = hd * 2 * PEER_DH
        s1 = _dot(keys_ref[2 * hd], qt[base:base + PEER_DH])
        s2 = _dot(keys_ref[2 * hd + 1], qt[base + PEER_DH:base + 2 * PEER_DH])
        a = _top_values(s1)
        b = _top_values(s2)
        rank2 = jnp.full(s2.shape, UNRANKED, F32)
        for jr in range(k, 0, -1):
            rank2 = jnp.where(s2 >= b[jr - 1:jr], float(jr), rank2)
        pieces = [a + b[0:1]]
        for j in range(2, 9):
            pieces.append(jnp.where(row8 < k // j, a[0:8] + b[j - 1:j], NEG_INF))
        pieces.append(a[0:1] + b[8:16])
        cand = jnp.concatenate(pieces, axis=0)
        top = a[0:1] + b[0:1]
        work = cand
        thr = top
        for r in range(k):
            thr = jnp.max(work, axis=0, keepdims=True)
            work = jnp.where(work == thr, NEG_INF, work)
        sel = cand >= thr
        zsum = jnp.sum(jnp.where(sel, jnp.exp(cand - top), 0.0), axis=0, keepdims=True)
        picked = sel.astype(F32)
        low = picked[16:24]
        for j in range(3, 9):
            low = low + picked[8 * j:8 * j + 8]
        cnt_tail = jnp.sum(picked[72:80], axis=0, keepdims=True)
        n_of_i = picked[0:16] + jnp.concatenate([low, zeros8], axis=0) + jnp.where(row16 == 0, cnt_tail, 0.0)
        n1 = jnp.zeros(s1.shape, F32)
        for i in range(k):
            n1 = jnp.where(s1 == a[i:i + 1], n_of_i[i:i + 1], n1)
        rows = slice(hd * PEER_NKEYS, (hd + 1) * PEER_NKEYS)
        rank2_ref[rows, :] = rank2.astype(BF16)
        e2_ref[rows, :] = (0.5 * jnp.exp(s2 - b[0:1]) / zsum).astype(BF16)
        n1_ref[rows, :] = n1
        e1_ref[rows, :] = jnp.exp(s1 - a[0:1])


def _route(x2d, gain, wqt, keys2, tt=256):
    m, d = x2d.shape
    rows = PEER_HEADS * PEER_NKEYS
    tab = pl.BlockSpec((rows, tt), lambda i: (0, i))
    return pl.pallas_call(
        _route_kernel,
        grid=(m // tt,),
        in_specs=[pl.BlockSpec((tt, d), lambda i: (i, 0)),
                  pl.BlockSpec((1, d), lambda i: (0, 0)),
                  pl.BlockSpec(wqt.shape, lambda i: (0, 0)),
                  pl.BlockSpec(keys2.shape, lambda i: (0, 0, 0))],
        out_specs=[pl.BlockSpec((d, tt), lambda i: (0, i)), tab, tab, tab, tab],
        out_shape=[jax.ShapeDtypeStruct((d, m), BF16)] + [jax.ShapeDtypeStruct((rows, m), BF16)] * 2
        + [jax.ShapeDtypeStruct((rows, m), F32)] * 2,
        compiler_params=_params(("parallel",)),
        name="peer_route",
    )(x2d, gain, wqt, keys2)


def _peer_kernel(final, x_ref, h_ref, rank2_ref, e2_ref, n1_ref, e1_ref, u_ref, vt_ref, gf_ref,
                 o_ref, acc_scr):
    j = pl.program_id(1)
    eb = u_ref.shape[0]
    nk = PEER_NKEYS

    @pl.when(j == 0)
    def _():
        acc_scr[...] = jnp.zeros_like(acc_scr)

    tt = h_ref.shape[1]
    half = tt // 2
    sizes = [s for s in PEER_CHUNK_BLOCKS]
    assert sum(sizes) * nk == eb
    starts = [sum(sizes[:c]) for c in range(len(sizes))]
    rows_of = [slice(st * nk, (st + sz) * nk) for st, sz in zip(starts, sizes)]

    def score(ck, th):
        return jnp.dot(u_ref[rows_of[ck], :], h_ref[:, th * half:(th + 1) * half],
                       preferred_element_type=F32).astype(BF16)

    def key_row(ref, hd, il, cols):
        r = ref[hd, il:il + 1, :][:, cols]
        return jnp.tile(jnp.broadcast_to(r, (BF16_ROWS, half)).astype(BF16), (nk // BF16_ROWS, 1))

    def act_tile(z_half, ck, sub, th):
        il = starts[ck] + sub
        cols = slice(th * half, (th + 1) * half)
        gate = None
        for hd in range(PEER_HEADS):
            rows = slice(hd * nk, (hd + 1) * nk)
            sel = jnp.where(rank2_ref[rows, cols] <= key_row(n1_ref, hd, il, cols), e2_ref[rows, cols], 0.0)
            term = sel * key_row(e1_ref, hd, il, cols)
            gate = term if gate is None else gate + term
        return _gelu2(z_half[sub * nk:(sub + 1) * nk, :]) * gate

    def apply(ck, th, act):
        cols = slice(th * half, (th + 1) * half)
        acc_scr[:, cols] += _dot(vt_ref[:, rows_of[ck]], act)

    z = [score(0, 0), score(0, 1)]
    prev = None
    for ck, n_sub in enumerate(sizes):
        pieces = []
        z_next = [None, None]
        if ck + 1 < len(sizes):
            pieces += [("score", 0), ("score", 1)]
        if prev is not None:
            pieces += [("apply", 0), ("apply", 1)]
        acts = ([], [])
        for sub in range(n_sub):
            for th in range(2):
                acts[th].append(act_tile(z[th], ck, sub, th))
            for kind, th in pieces[sub * len(pieces) // n_sub:(sub + 1) * len(pieces) // n_sub]:
                if kind == "score":
                    z_next[th] = score(ck + 1, th)
                else:
                    apply(ck - 1, th, prev[th])
        prev = [jnp.concatenate(acts[0], axis=0), jnp.concatenate(acts[1], axis=0)]
        z = z_next
    apply(len(sizes) - 1, 0, prev[0])
    apply(len(sizes) - 1, 1, prev[1])

    @pl.when(j == pl.num_programs(1) - 1)
    def _():
        y = x_ref[...] + acc_scr[...].T
        if final:
            y = _rms(y, gf_ref[...])
        o_ref[...] = y


def _peer(x2d, h, rank2, e2, n1, e1, u_layers, vt_layers, layer, gfinal, final, tt=512, eb=2048):
    m, d = x2d.shape
    ne = u_layers.shape[1]
    rows = PEER_HEADS * PEER_NKEYS
    tab = pl.BlockSpec((rows, tt), lambda i, j: (0, i))
    key_tab = pl.BlockSpec((PEER_HEADS, eb // PEER_NKEYS, tt), lambda i, j: (0, j, i))
    n1, e1 = (a.reshape(PEER_HEADS, PEER_NKEYS, m) for a in (n1, e1))
    return pl.pallas_call(
        functools.partial(_peer_kernel, final),
        grid=(m // tt, ne // eb),
        in_specs=[pl.BlockSpec((tt, d), lambda i, j: (i, 0)),
                  pl.BlockSpec((d, tt), lambda i, j: (0, i)),
                  tab, tab, key_tab, key_tab,
                  pl.BlockSpec((None, eb, d), lambda i, j: (layer, j, 0)),
                  pl.BlockSpec((None, d, eb), lambda i, j: (layer, 0, j)),
                  pl.BlockSpec((1, d), lambda i, j: (0, 0))],
        out_specs=pl.BlockSpec((tt, d), lambda i, j: (i, 0)),
        out_shape=jax.ShapeDtypeStruct((m, d), F32),
        scratch_shapes=[pltpu.VMEM((d, tt), F32)],
        compiler_params=_params(("parallel", "arbitrary")),
        name="peer_experts",
    )(x2d, h, rank2, e2, n1, e1, u_layers, vt_layers, gfinal)


def kernel(x, mem, positions, norm_mix, w_in, b_gate, hgrn_lb, hgrn_norm, s5_lambda_re, s5_lambda_im,
           s5_log_step, s5_b_re, s5_b_im, s5_c_re, s5_c_im, s5_d, s5_w_glu, s5_b_glu, w_branch, w_out,
           norm_cross, norm_mem, xa_wq, xa_wkv, xa_wo, norm_ffn, peer_wq, peer_keys, peer_u, peer_v,
           norm_final):
    bsz, length, d = x.shape
    depth = w_in.shape[0]
    row = lambda a: a.astype(F32).reshape(1, -1)
    x2d = x.astype(F32).reshape(bsz * length, d)
    cos_t, sin_t = _rope_tables(positions)
    lb = hgrn_lb.astype(F32)
    w_in_b = w_in.astype(BF16)
    peer_u_b = peer_u.astype(BF16)
    peer_vt_b = peer_v.astype(BF16).transpose(0, 2, 1)
    for l in range(depth):
        proj = _inproj(x2d, row(norm_mix[l]), w_in_b, l)
        proj3 = proj.reshape(bsz, length, IN_COLS)
        o_hg, o_rt = _hgrn_retention(proj3, lb, row(hgrn_norm[l]), l, cos_t, sin_t)
        bb_re, bb_im, tab = _s5_prep(s5_lambda_re[l], s5_lambda_im[l], s5_log_step[l], s5_b_re[l], s5_b_im[l])
        bbd = jnp.concatenate([_block_diag_in(bb_re), _block_diag_in(bb_im)], axis=2).astype(BF16)
        o_s5 = _s5(proj3, bbd, tab, _block_diag_out(s5_c_re[l]).astype(BF16),
                   _block_diag_out(s5_c_im[l]).astype(BF16), row(s5_d[l]),
                   s5_w_glu[l].astype(BF16), row(s5_b_glu[l]))
        kmem, vmem = _memkv(mem.astype(F32), row(norm_mem[l]), xa_wkv[l].astype(BF16))
        x2d = _mix(x2d, proj, row(b_gate[l]), o_hg.reshape(-1, MIX), o_s5.reshape(-1, S5_WIDTH),
                   o_rt.reshape(-1, MIX), w_branch[l].astype(BF16), w_out[l].astype(BF16),
                   row(norm_cross[l]), kmem, vmem, xa_wq[l].astype(BF16), xa_wo[l].astype(BF16), length)
        keys2 = peer_keys[l].astype(BF16).reshape(PEER_HEADS * 2, PEER_NKEYS, PEER_DH)
        h, rank2, e2, n1, e1 = _route(x2d, row(norm_ffn[l]), peer_wq[l].astype(BF16).T, keys2)
        x2d = _peer(x2d, h, rank2, e2, n1, e1, peer_u_b, peer_vt_b, l, row(norm_final), final=(l == depth - 1))
    return x2d.reshape(bsz, length, d)
```
